```python
import math
import jax
import jax.numpy as jnp
from jax import lax
import numpy as np

D_MODEL = 1024
BATCH = 16
SEQ = 4096
DEPTH = 4

N_MIXERS = 3
N_HEADS = 16
HEAD_DIM = 64
Q_DIM = N_HEADS * HEAD_DIM
N_KV_HEADS = 4
GQA_GROUP = N_HEADS // N_KV_HEADS
WINDOW = 128
BLOCK = 128
GMLP_WIDTH = D_MODEL
GMLP_CHUNK = 128
GMLP_GROUPS = 8
GMLP_GROUP_DIM = GMLP_WIDTH // GMLP_GROUPS
KV_RANK = 128
IDX_HEADS = 8
IDX_DIM = 64
TOPK_MAX = 256
REL_BUCKETS = 32
REL_MAX_DIST = 128
D_FF = 2816
CONV_WIDTH = 3
EPS = 1e-6
SWA_IN = (N_HEADS + 2 * N_KV_HEADS) * HEAD_DIM
DSA_SPLITS = (Q_DIM, Q_DIM + KV_RANK, Q_DIM + KV_RANK + IDX_HEADS * IDX_DIM, Q_DIM + KV_RANK + IDX_HEADS * IDX_DIM + IDX_DIM)
DSA_IN = DSA_SPLITS[3] + IDX_HEADS

kernel_name = 'hybrid_swa_gmlp_dsa_convffn'


def rms_norm(x, g):
    xf = x.astype(jnp.float32)
    y = xf * lax.rsqrt(jnp.mean(xf * xf, axis=-1, keepdims=True) + EPS)
    return (y * g.astype(jnp.float32)).astype(x.dtype)


def layer_norm(x, g, b):
    xf = x.astype(jnp.float32)
    mu = jnp.mean(xf, axis=-1, keepdims=True)
    var = jnp.mean(jnp.square(xf - mu), axis=-1, keepdims=True)
    y = (xf - mu) * lax.rsqrt(var + EPS) * g.astype(jnp.float32) + b.astype(jnp.float32)
    return y.astype(x.dtype)


def rel_bucket(dist):
    max_exact = REL_BUCKETS // 2
    d = jnp.maximum(dist, 0)
    df = jnp.maximum(d, 1).astype(jnp.float32)
    large = max_exact + (jnp.log(df / max_exact) / math.log(REL_MAX_DIST / max_exact) * (REL_BUCKETS - max_exact)).astype(jnp.int32)
    large = jnp.minimum(large, REL_BUCKETS - 1)
    return jnp.where(d < max_exact, d, large)


def to_blocks(t):
    b, s = t.shape[:2]
    return jnp.moveaxis(t.reshape((b, s // BLOCK, BLOCK) + t.shape[2:]), 1, 0)


def from_blocks(t):
    nb, b = t.shape[:2]
    return jnp.moveaxis(t, 0, 1).reshape((b, nb * BLOCK) + t.shape[3:])


def swa_sink_attention(x, w_in, sinks, w_out, rel_bias):
    b, s, _ = x.shape
    nb = s // BLOCK
    q, k, v = jnp.split(x @ w_in, [Q_DIM, Q_DIM + N_KV_HEADS * HEAD_DIM], axis=-1)
    q = q.reshape(b, s, N_KV_HEADS, GQA_GROUP, HEAD_DIM)

    def with_prev(t):
        t = t.reshape(b, nb, BLOCK, N_KV_HEADS, HEAD_DIM)
        prev = jnp.pad(t, ((0, 0), (1, 0), (0, 0), (0, 0), (0, 0)))[:, :-1]
        return jnp.moveaxis(jnp.concatenate([prev, t], axis=2), 1, 0)

    kw, vw = with_prev(k), with_prev(v)
    q_pos = jnp.arange(BLOCK)[:, None] + BLOCK
    k_pos = jnp.arange(2 * BLOCK)[None, :]
    dist = q_pos - k_pos
    band = (dist >= 0) & (dist < WINDOW)
    first = (jnp.arange(nb) == 0)[:, None, None]
    mask = band[None] & ~(first & (k_pos < BLOCK)[None])
    bias = rel_bias[rel_bucket(dist)].astype(jnp.float32)
    bias = bias.reshape(BLOCK, 2 * BLOCK, N_KV_HEADS, GQA_GROUP).transpose(2, 3, 0, 1)
    sink = sinks.astype(jnp.float32).reshape(N_KV_HEADS, GQA_GROUP)[..., None, None]
    scale = HEAD_DIM ** -0.5

    def block_attn(args):
        qb, kb, vb, mb = args
        sc = jnp.einsum('bqkgd,bskd->bkgqs', qb, kb).astype(jnp.float32) * scale + bias
        sc = jnp.where(mb, sc, -jnp.inf)
        m = jnp.maximum(jnp.max(sc, axis=-1, keepdims=True), sink)
        p = jnp.exp(sc - m)
        p = p / (jnp.sum(p, axis=-1, keepdims=True) + jnp.exp(sink - m))
        return jnp.einsum('bkgqs,bskd->bqkgd', p.astype(vb.dtype), vb)

    o = lax.map(block_attn, (to_blocks(q), kw, vw, mask))
    return from_blocks(o).reshape(b, s, Q_DIM) @ w_out


def chunked_gmlp(x, w_in, ln_g, ln_b, w_s, b_s, w_out):
    b, s, _ = x.shape
    nc = s // GMLP_CHUNK
    u, v = jnp.split(jax.nn.gelu(x @ w_in), 2, axis=-1)
    v = layer_norm(v, ln_g, ln_b).reshape(b, nc, GMLP_CHUNK, GMLP_GROUPS, GMLP_GROUP_DIM)
    causal = jnp.tril(jnp.ones((GMLP_CHUNK, GMLP_CHUNK), dtype=bool))
    ws = jnp.where(causal, w_s, jnp.zeros_like(w_s))
    mixed = jnp.einsum('gts,bnsgc->bntgc', ws, v) + b_s.T[:, :, None]
    return (u * mixed.reshape(b, s, GMLP_WIDTH)) @ w_out


def dsa_attention(x, w_in, kv_norm_g, kidx_norm_g, w_uk, w_uv, w_out, rel_bias):
    b, s, _ = x.shape
    top_k = min(TOPK_MAX, s // 4)
    q, c_kv, q_idx, k_idx, w_idx = jnp.split(x @ w_in, list(DSA_SPLITS), axis=-1)
    q = q.reshape(b, s, N_HEADS, HEAD_DIM)
    c_kv = rms_norm(c_kv, kv_norm_g)
    q_abs = jnp.einsum('bshd,rhd->bshr', q, w_uk)
    q_idx = q_idx.reshape(b, s, IDX_HEADS, IDX_DIM)
    k_idx = rms_norm(k_idx, kidx_norm_g)
    w_idx = w_idx * (IDX_HEADS ** -0.5 * IDX_DIM ** -0.5)
    key_pos = jnp.arange(s)
    scale = HEAD_DIM ** -0.5

    def block(args):
        n, qa, qi, wi = args
        q_pos = n * BLOCK + jnp.arange(BLOCK)
        sc = jax.nn.relu(jnp.einsum('bqhd,bsd->bqhs', qi, k_idx))
        sc = jnp.einsum('bqh,bqhs->bqs', wi, sc).astype(jnp.float32)
        sc = jnp.where(key_pos[None, :] <= q_pos[:, None], sc, -jnp.inf)
        _, idx = lax.top_k(sc, top_k)
        c_sel = jax.vmap(lambda c, i: c[i])(c_kv, idx)
        dist = q_pos[None, :, None] - idx
        bias = jnp.moveaxis(rel_bias[rel_bucket(dist)], -1, 1).astype(jnp.float32)
        logits = jnp.einsum('bqhr,bqkr->bhqk', qa, c_sel).astype(jnp.float32) * scale + bias
        logits = jnp.where((dist >= 0)[:, None], logits, -jnp.inf)
        p = jax.nn.softmax(logits, axis=-1).astype(c_sel.dtype)
        return jnp.einsum('bhqk,bqkr->bqhr', p, c_sel)

    nb = s // BLOCK
    o_lat = from_blocks(lax.map(block, (jnp.arange(nb), to_blocks(q_abs), to_blocks(q_idx), to_blocks(w_idx))))
    o = jnp.einsum('bshr,rhd->bshd', o_lat, w_uv).reshape(b, s, Q_DIM)
    return o @ w_out


def conv_ffn(x, w_up, conv_w, conv_b, w_down):
    h = x @ w_up
    ch = h.shape[-1]
    h = lax.conv_general_dilated(h, conv_w[:, None, :], window_strides=(1,), padding=((CONV_WIDTH - 1, 0),), dimension_numbers=('NWC', 'WIO', 'NWC'), feature_group_count=ch) + conv_b
    gate, val = jnp.split(h, 2, axis=-1)
    return (jax.nn.silu(gate) * val) @ w_down


def setup_inputs(seed: int = 0) -> dict:
    key = jax.random.key(seed)
    ks = list(jax.random.split(key, 128))

    def nrm(shape, scale):
        return jax.random.normal(ks.pop(), shape, jnp.float32) * scale

    def gain(n):
        return 1.0 + nrm((n,), 0.02)

    inp = {}
    inp['x'] = nrm((BATCH, SEQ, D_MODEL), 1.0)
    inp['rel_bias'] = nrm((REL_BUCKETS, N_HEADS), 0.5)
    for i in range(DEPTH):
        p = 'l%d_' % i
        inp[p + 'norm_pre_mix'] = gain(D_MODEL)
        inp[p + 'norm_post_mix'] = gain(D_MODEL)
        kind = i % N_MIXERS
        if kind == 0:
            inp[p + 'attn_w_in'] = nrm((D_MODEL, SWA_IN), D_MODEL ** -0.5)
            inp[p + 'attn_sinks'] = nrm((N_HEADS,), 0.5)
            inp[p + 'attn_w_out'] = nrm((Q_DIM, D_MODEL), Q_DIM ** -0.5)
        elif kind == 1:
            inp[p + 'gmlp_w_in'] = nrm((D_MODEL, 2 * GMLP_WIDTH), D_MODEL ** -0.5)
            inp[p + 'gmlp_ln_g'] = gain(GMLP_WIDTH)
            inp[p + 'gmlp_ln_b'] = nrm((GMLP_WIDTH,), 0.02)
            inp[p + 'gmlp_w_s'] = nrm((GMLP_GROUPS, GMLP_CHUNK, GMLP_CHUNK), GMLP_CHUNK ** -0.5)
            inp[p + 'gmlp_b_s'] = 1.0 + nrm((GMLP_GROUPS, GMLP_CHUNK), 0.02)
            inp[p + 'gmlp_w_out'] = nrm((GMLP_WIDTH, D_MODEL), GMLP_WIDTH ** -0.5)
        else:
            inp[p + 'dsa_w_in'] = nrm((D_MODEL, DSA_IN), D_MODEL ** -0.5)
            inp[p + 'dsa_kv_norm'] = gain(KV_RANK)
            inp[p + 'dsa_kidx_norm'] = gain(IDX_DIM)
            inp[p + 'dsa_w_uk'] = nrm((KV_RANK, N_HEADS, HEAD_DIM), KV_RANK ** -0.5)
            inp[p + 'dsa_w_uv'] = nrm((KV_RANK, N_HEADS, HEAD_DIM), KV_RANK ** -0.5)
            inp[p + 'dsa_w_out'] = nrm((Q_DIM, D_MODEL), Q_DIM ** -0.5)
        inp[p + 'norm_pre_ffn'] = gain(D_MODEL)
        inp[p + 'norm_post_ffn'] = gain(D_MODEL)
        inp[p + 'ffn_w_up'] = nrm((D_MODEL, 2 * D_FF), D_MODEL ** -0.5)
        inp[p + 'ffn_conv_w'] = nrm((CONV_WIDTH, 2 * D_FF), CONV_WIDTH ** -0.5)
        inp[p + 'ffn_conv_b'] = nrm((2 * D_FF,), 0.02)
        inp[p + 'ffn_w_down'] = nrm((D_FF, D_MODEL), D_FF ** -0.5)
    return inp


def reference(x, rel_bias,
              l0_norm_pre_mix, l0_norm_post_mix, l0_attn_w_in, l0_attn_sinks, l0_attn_w_out,
              l0_norm_pre_ffn, l0_norm_post_ffn, l0_ffn_w_up, l0_ffn_conv_w, l0_ffn_conv_b, l0_ffn_w_down,
              l1_norm_pre_mix, l1_norm_post_mix, l1_gmlp_w_in, l1_gmlp_ln_g, l1_gmlp_ln_b, l1_gmlp_w_s, l1_gmlp_b_s, l1_gmlp_w_out,
              l1_norm_pre_ffn, l1_norm_post_ffn, l1_ffn_w_up, l1_ffn_conv_w, l1_ffn_conv_b, l1_ffn_w_down,
              l2_norm_pre_mix, l2_norm_post_mix, l2_dsa_w_in, l2_dsa_kv_norm, l2_dsa_kidx_norm, l2_dsa_w_uk, l2_dsa_w_uv, l2_dsa_w_out,
              l2_norm_pre_ffn, l2_norm_post_ffn, l2_ffn_w_up, l2_ffn_conv_w, l2_ffn_conv_b, l2_ffn_w_down,
              l3_norm_pre_mix, l3_norm_post_mix, l3_attn_w_in, l3_attn_sinks, l3_attn_w_out,
              l3_norm_pre_ffn, l3_norm_post_ffn, l3_ffn_w_up, l3_ffn_conv_w, l3_ffn_conv_b, l3_ffn_w_down):
    mixers = [
        (l0_attn_w_in, l0_attn_sinks, l0_attn_w_out),
        (l1_gmlp_w_in, l1_gmlp_ln_g, l1_gmlp_ln_b, l1_gmlp_w_s, l1_gmlp_b_s, l1_gmlp_w_out),
        (l2_dsa_w_in, l2_dsa_kv_norm, l2_dsa_kidx_norm, l2_dsa_w_uk, l2_dsa_w_uv, l2_dsa_w_out),
        (l3_attn_w_in, l3_attn_sinks, l3_attn_w_out),
    ]
    norms = [
        (l0_norm_pre_mix, l0_norm_post_mix, l0_norm_pre_ffn, l0_norm_post_ffn),
        (l1_norm_pre_mix, l1_norm_post_mix, l1_norm_pre_ffn, l1_norm_post_ffn),
        (l2_norm_pre_mix, l2_norm_post_mix, l2_norm_pre_ffn, l2_norm_post_ffn),
        (l3_norm_pre_mix, l3_norm_post_mix, l3_norm_pre_ffn, l3_norm_post_ffn),
    ]
    ffns = [
        (l0_ffn_w_up, l0_ffn_conv_w, l0_ffn_conv_b, l0_ffn_w_down),
        (l1_ffn_w_up, l1_ffn_conv_w, l1_ffn_conv_b, l1_ffn_w_down),
        (l2_ffn_w_up, l2_ffn_conv_w, l2_ffn_conv_b, l2_ffn_w_down),
        (l3_ffn_w_up, l3_ffn_conv_w, l3_ffn_conv_b, l3_ffn_w_down),
    ]
    for i in range(DEPTH):
        pre_mix, post_mix, pre_ffn, post_ffn = norms[i]
        h = rms_norm(x, pre_mix)
        kind = i % N_MIXERS
        if kind == 0:
            m = swa_sink_attention(h, *mixers[i], rel_bias)
        elif kind == 1:
            m = chunked_gmlp(h, *mixers[i])
        else:
            m = dsa_attention(h, *mixers[i], rel_bias)
        x = x + rms_norm(m, post_mix)
        x = x + rms_norm(conv_ffn(rms_norm(x, pre_ffn), *ffns[i]), post_ffn)
    return x
```

```python
import functools
import math

import jax
import jax.numpy as jnp
from jax import lax
from jax.experimental import pallas as pl
from jax.experimental.pallas import tpu as pltpu

F32 = jnp.float32
BF16 = jnp.bfloat16

EPS = 1e-6
N_HEADS = 16
HEAD_DIM = 64
N_KV_HEADS = 4
GQA_GROUP = N_HEADS // N_KV_HEADS
WINDOW = 128
BLOCK = 128
GMLP_CHUNK = 128
GMLP_GROUPS = 8
KV_RANK = 128
IDX_HEADS = 8
IDX_DIM = 64
TOPK_MAX = 256
REL_BUCKETS = 32
REL_MAX_DIST = 128
CONV_WIDTH = 3

V7X_LANES = 128
V7X_BF16_SUBLANES = 16
V7X_VMEM_LIMIT_BYTES = 56 * 1024 * 1024


def _rms(xf, g):
    ms = jnp.mean(xf * xf, axis=-1, keepdims=True)
    return xf * lax.rsqrt(ms + EPS) * g


def _resident(shape):
    nd = len(shape)
    return pl.BlockSpec(shape, lambda *_: (0,) * nd, pipeline_mode=pl.Buffered(1))


FFN_TM = 512
FFN_TN = 256
FFN_HALO = V7X_BF16_SUBLANES


def _ffn_body(x_ref, xh_ref, pre_ref, post_ref, wg_ref, wv_ref, cw_ref, cb_ref, wd_ref,
              o_ref, xn_sc, *, tm, nch, halo):
    i = pl.program_id(1)
    x = x_ref[...]
    pre = pre_ref[...]
    xn_sc[halo:, :] = _rms(x, pre).astype(BF16)
    xh = jnp.where(i > 0, _rms(xh_ref[...], pre), 0.0)
    xn_sc[:halo, :] = xh.astype(BF16)
    xn = xn_sc[...]

    def conv(h, w, b):
        h1 = pltpu.roll(h, 1, axis=0)
        h2 = pltpu.roll(h, 2, axis=0)
        y = h * w[2:3, :] + h1 * w[1:2, :] + h2 * w[0:1, :] + b
        return y[halo:, :]

    acc = jnp.zeros((tm, x.shape[-1]), F32)
    for j in range(nch):
        hg = jnp.dot(xn, wg_ref[j], preferred_element_type=F32)
        hv = jnp.dot(xn, wv_ref[j], preferred_element_type=F32)
        g = conv(hg, cw_ref[j], cb_ref[j])
        v = conv(hv, cw_ref[nch + j], cb_ref[nch + j])
        a = g * (1.0 / (1.0 + jnp.exp(-g))) * v
        acc = acc + jnp.dot(a.astype(BF16), wd_ref[j], preferred_element_type=F32)
    o_ref[...] = x + _rms(acc, post_ref[...])


def _ffn(x, pre_g, post_g, w_up, conv_w, conv_b, w_down):
    B, S, D = x.shape
    dff = w_down.shape[0]
    tm, tn, halo = min(FFN_TM, S), FFN_TN, FFN_HALO
    nch = dff // tn
    assert dff % tn == 0 and S % tm == 0 and tm % halo == 0
    wup = w_up.astype(BF16).reshape(D, 2, nch, tn)
    wg = jnp.transpose(wup[:, 0], (1, 0, 2))
    wv = jnp.transpose(wup[:, 1], (1, 0, 2))
    wd = w_down.astype(BF16).reshape(nch, tn, D)
    cw = jnp.transpose(conv_w.reshape(CONV_WIDTH, 2 * nch, tn), (1, 0, 2))
    cb = conv_b.reshape(2 * nch, 1, tn)
    body = functools.partial(_ffn_body, tm=tm, nch=nch, halo=halo)
    return pl.pallas_call(
        body,
        out_shape=jax.ShapeDtypeStruct((B, S, D), F32),
        grid=(B, S // tm),
        in_specs=[
            pl.BlockSpec((None, tm, D), lambda b, i: (b, i, 0)),
            pl.BlockSpec((None, halo, D), lambda b, i: (b, jnp.maximum(i * (tm // halo) - 1, 0), 0)),
            _resident((1, D)), _resident((1, D)),
            _resident((nch, D, tn)), _resident((nch, D, tn)),
            _resident((2 * nch, CONV_WIDTH, tn)), _resident((2 * nch, 1, tn)),
            _resident((nch, tn, D)),
        ],
        out_specs=pl.BlockSpec((None, tm, D), lambda b, i: (b, i, 0)),
        scratch_shapes=[pltpu.VMEM((halo + tm, D), BF16)],
        compiler_params=pltpu.CompilerParams(
            dimension_semantics=("parallel", "parallel"),
            vmem_limit_bytes=V7X_VMEM_LIMIT_BYTES),
        name="ffn",
    )(x, x, pre_g.reshape(1, D), post_g.reshape(1, D), wg, wv, cw, cb, wd)


def _rel_bucket(dist):
    max_exact = REL_BUCKETS // 2
    d = jnp.maximum(dist, 0)
    df = jnp.maximum(d, 1).astype(F32)
    large = max_exact + (jnp.log(df / max_exact) / math.log(REL_MAX_DIST / max_exact)
                         * (REL_BUCKETS - max_exact)).astype(jnp.int32)
    large = jnp.minimum(large, REL_BUCKETS - 1)
    return jnp.where(d < max_exact, d, large)


SWA_TQ = 256


def _swa_body(x_ref, pre_ref, post_ref, win_ref, sink_ref, bias_ref, wout_ref, o_ref,
              kv_sc, o_sc, *, tq):
    i = pl.program_id(1)
    qdim = N_HEADS * HEAD_DIM
    kdim = N_KV_HEADS * HEAD_DIM

    @pl.when(i == 0)
    def _():
        kv_sc[0] = jnp.zeros(kv_sc.shape[1:], BF16)

    x = x_ref[...]
    xn = _rms(x, pre_ref[...]).astype(BF16)
    qkv = jnp.dot(xn, win_ref[...], preferred_element_type=F32)
    q = (qkv[:, :qdim] * (HEAD_DIM ** -0.5)).astype(BF16)
    kv = qkv[:, qdim:].astype(BF16)
    col = lax.broadcasted_iota(jnp.int32, (BLOCK, 2 * BLOCK), 1)
    first_masked = jnp.where(i == 0, BLOCK, 0)
    for r in range(tq // BLOCK):
        rows = slice(r * BLOCK, (r + 1) * BLOCK)
        kv_prev = kv_sc[i % 2] if r == 0 else kv[(r - 1) * BLOCK:r * BLOCK]
        kvw = jnp.concatenate([kv_prev, kv[rows]], axis=0)
        for kvh in range(N_KV_HEADS):
            kh = kvw[:, kvh * HEAD_DIM:(kvh + 1) * HEAD_DIM]
            vh = kvw[:, kdim + kvh * HEAD_DIM:kdim + (kvh + 1) * HEAD_DIM]
            for g in range(GQA_GROUP):
                h = kvh * GQA_GROUP + g
                qh = q[rows, h * HEAD_DIM:(h + 1) * HEAD_DIM]
                s = lax.dot_general(qh, kh, (((1,), (1,)), ((), ())),
                                    preferred_element_type=F32) + bias_ref[h]
                if r == 0:
                    s = jnp.where(col < first_masked, -jnp.inf, s)
                sink = sink_ref[h]
                m = jnp.maximum(jnp.max(s, axis=-1, keepdims=True), sink)
                p = jnp.exp(s - m)
                den = jnp.sum(p, axis=-1, keepdims=True) + jnp.exp(sink - m)
                oh = jnp.dot(p.astype(BF16), vh, preferred_element_type=F32) * (1.0 / den)
                o_sc[rows, h * HEAD_DIM:(h + 1) * HEAD_DIM] = oh.astype(BF16)
    kv_sc[(i + 1) % 2] = kv[tq - BLOCK:]
    mo = jnp.dot(o_sc[...], wout_ref[...], preferred_element_type=F32)
    o_ref[...] = x + _rms(mo, post_ref[...])


def _swa(x, pre_g, post_g, w_in, sinks, w_out, rel_bias):
    B, S, D = x.shape
    tq = min(SWA_TQ, S)
    qdim, kdim = N_HEADS * HEAD_DIM, N_KV_HEADS * HEAD_DIM
    assert S % tq == 0 and tq % BLOCK == 0
    dist = (jnp.arange(BLOCK)[:, None] + BLOCK) - jnp.arange(2 * BLOCK)[None, :]
    band = (dist >= 0) & (dist < WINDOW)
    bias = jnp.transpose(rel_bias[_rel_bucket(dist)].astype(F32), (2, 0, 1))
    bias = jnp.where(band[None], bias, -jnp.inf)
    body = functools.partial(_swa_body, tq=tq)
    return pl.pallas_call(
        body,
        out_shape=jax.ShapeDtypeStruct((B, S, D), F32),
        grid=(B, S // tq),
        in_specs=[
            pl.BlockSpec((None, tq, D), lambda b, i: (b, i, 0)),
            _resident((1, D)), _resident((1, D)),
            _resident((D, qdim + 2 * kdim)),
            pl.BlockSpec(memory_space=pltpu.SMEM),
            _resident((N_HEADS, BLOCK, 2 * BLOCK)),
            _resident((qdim, D)),
        ],
        out_specs=pl.BlockSpec((None, tq, D), lambda b, i: (b, i, 0)),
        scratch_shapes=[pltpu.VMEM((2, BLOCK, 2 * kdim), BF16), pltpu.VMEM((tq, qdim), BF16)],
        compiler_params=pltpu.CompilerParams(
            dimension_semantics=("arbitrary", "arbitrary"),
            vmem_limit_bytes=V7X_VMEM_LIMIT_BYTES),
        name="swa",
    )(x, pre_g.reshape(1, D), post_g.reshape(1, D), w_in.astype(BF16), sinks.astype(F32),
      bias, w_out.astype(BF16))


GMLP_TQ = 256


def _gmlp_body(x_ref, pre_ref, post_ref, win_ref, lng_ref, lnb_ref, ws_ref, bs_ref, wout_ref,
               o_ref, gated_sc, *, tq):
    x = x_ref[...]
    width = wout_ref.shape[0]
    gdim = width // GMLP_GROUPS
    xn = _rms(x, pre_ref[...]).astype(BF16)
    h = jax.nn.gelu(jnp.dot(xn, win_ref[...], preferred_element_type=F32))
    u, v = h[:, :width], h[:, width:]
    mu = jnp.mean(v, axis=-1, keepdims=True)
    var = jnp.mean(jnp.square(v - mu), axis=-1, keepdims=True)
    vn = ((v - mu) * lax.rsqrt(var + EPS) * lng_ref[...] + lnb_ref[...]).astype(BF16)
    for c in range(tq // GMLP_CHUNK):
        rows = slice(c * GMLP_CHUNK, (c + 1) * GMLP_CHUNK)
        for g in range(GMLP_GROUPS):
            cols = slice(g * gdim, (g + 1) * gdim)
            mixed = jnp.dot(ws_ref[g], vn[rows, cols], preferred_element_type=F32) + bs_ref[g]
            gated_sc[rows, cols] = (u[rows, cols] * mixed).astype(BF16)
    mo = jnp.dot(gated_sc[...], wout_ref[...], preferred_element_type=F32)
    o_ref[...] = x + _rms(mo, post_ref[...])


def _gmlp(x, pre_g, post_g, w_in, ln_g, ln_b, w_s, b_s, w_out):
    B, S, D = x.shape
    width = w_out.shape[0]
    gdim = width // GMLP_GROUPS
    tq = min(GMLP_TQ, S)
    assert S % tq == 0 and tq % GMLP_CHUNK == 0 and gdim == V7X_LANES
    causal = jnp.tril(jnp.ones((GMLP_CHUNK, GMLP_CHUNK), dtype=bool))
    ws = jnp.where(causal, w_s, jnp.zeros_like(w_s)).astype(BF16)
    bs = jnp.broadcast_to(b_s[:, :, None], (GMLP_GROUPS, GMLP_CHUNK, gdim)).astype(F32)
    body = functools.partial(_gmlp_body, tq=tq)
    return pl.pallas_call(
        body,
        out_shape=jax.ShapeDtypeStruct((B, S, D), F32),
        grid=(B, S // tq),
        in_specs=[
            pl.BlockSpec((None, tq, D), lambda b, i: (b, i, 0)),
            _resident((1, D)), _resident((1, D)),
            _resident((D, 2 * width)),
            _resident((1, width)), _resident((1, width)),
            _resident((GMLP_GROUPS, GMLP_CHUNK, GMLP_CHUNK)),
            _resident((GMLP_GROUPS, GMLP_CHUNK, gdim)),
            _resident((width, D)),
        ],
        out_specs=pl.BlockSpec((None, tq, D), lambda b, i: (b, i, 0)),
        scratch_shapes=[pltpu.VMEM((tq, width), BF16)],
        compiler_params=pltpu.CompilerParams(
            dimension_semantics=("parallel", "parallel"),
            vmem_limit_bytes=V7X_VMEM_LIMIT_BYTES),
        name="gmlp",
    )(x, pre_g.reshape(1, D), post_g.reshape(1, D), w_in.astype(BF16),
      ln_g.reshape(1, width), ln_b.reshape(1, width), ws, bs, w_out.astype(BF16))


DSA_TQ = 256
DSA_KC = 256
DSA_SMALL = 256
INT_MIN = -2 ** 31
BIAS_ROWS = 3 * DSA_KC
NEG_BIG = -1e30


def _dsa_proj_body(x_ref, pre_ref, wq_ref, wqi_ref, wsm_ref, kvg_ref, kig_ref, wuk_ref,
                   qabs_ref, ckv_ref, ckvt_ref, qidx_ref, kidx_ref, widxt_ref, *, tq):
    xn = _rms(x_ref[...], pre_ref[...]).astype(BF16)
    q = jnp.dot(xn, wq_ref[...], preferred_element_type=F32).astype(BF16)
    for h in range(N_HEADS):
        qa = jnp.dot(q[:, h * HEAD_DIM:(h + 1) * HEAD_DIM], wuk_ref[h],
                     preferred_element_type=F32) * (HEAD_DIM ** -0.5)
        qa = qa.astype(BF16)
        for blk in range(tq // BLOCK):
            qabs_ref[blk, h * BLOCK:(h + 1) * BLOCK, :] = qa[blk * BLOCK:(blk + 1) * BLOCK]
    qidx_ref[...] = jnp.dot(xn, wqi_ref[...], preferred_element_type=F32).astype(BF16)
    sm = jnp.dot(xn, wsm_ref[...], preferred_element_type=F32)
    ckv = _rms(sm[:, :KV_RANK], kvg_ref[...])
    ckv_ref[...] = ckv.astype(BF16)
    for cc in range(tq // DSA_KC):
        ckvt_ref[cc] = jnp.transpose(ckv[cc * DSA_KC:(cc + 1) * DSA_KC]).astype(BF16)
    kidx_ref[...] = _rms(sm[:, KV_RANK:KV_RANK + IDX_DIM], kig_ref[...]).astype(BF16)
    tail = jnp.transpose(sm[:, KV_RANK:2 * KV_RANK])
    widxt_ref[...] = tail[IDX_DIM:IDX_DIM + IDX_HEADS, :] * (IDX_HEADS ** -0.5 * IDX_DIM ** -0.5)


def _dsa_attn_body(x_ref, qabs_ref, qidx_ref, widxt_ref, kidx_ref, ckv_ref, ckvt_ref, bias_ref,
                   wuv_ref, wout_ref, post_ref, o_ref,
                   keys_sc, acc_sc, m_sc, l_sc, o_sc, *, top_k):
    n = pl.program_id(1)
    kc = DSA_KC
    nch = (n + 2) // 2
    lane = lax.broadcasted_iota(jnp.int32, (kc, BLOCK), 1)
    row = lax.broadcasted_iota(jnp.int32, (kc, BLOCK), 0)
    qpos = n * BLOCK + lane
    nt = (((1,), (1,)), ((), ()))

    def p1(c, carry):
        kblk = kidx_ref[pl.ds(pl.multiple_of(c * kc, kc), kc), :]
        acc = jnp.zeros((kc, BLOCK), F32)
        for h in range(IDX_HEADS):
            xt = lax.dot_general(kblk, qidx_ref[:, h * IDX_DIM:(h + 1) * IDX_DIM], nt,
                                 preferred_element_type=F32)
            acc = acc + widxt_ref[h:h + 1, :] * jnp.maximum(xt, 0.0)
        bits = pltpu.bitcast(acc, jnp.int32)
        key = bits ^ ((bits >> 31) & 0x7FFFFFFF)
        key = jnp.where(c * kc + row <= qpos, key, INT_MIN)
        keys_sc[pl.ds(pl.multiple_of(c * kc, kc), kc), :] = key
        return carry

    lax.fori_loop(0, nch, p1, 0)

    def count_ge(cand):
        def body(c, cnt):
            k = keys_sc[pl.ds(pl.multiple_of(c * kc, kc), kc), :]
            hit = jnp.where(k >= cand, jnp.int32(1), jnp.int32(0))
            return cnt + jnp.sum(hit.reshape(kc // 8, 8, BLOCK), axis=0, dtype=jnp.int32)
        cnt = lax.fori_loop(0, nch, body, jnp.zeros((8, BLOCK), jnp.int32))
        return jnp.sum(cnt, axis=0, keepdims=True, dtype=jnp.int32)

    def bisect(it, prefix):
        cand = prefix ^ jnp.left_shift(jnp.int32(1), 31 - it)
        return jnp.where(count_ge(cand) >= top_k, cand, prefix)

    thr = lax.fori_loop(0, 32, bisect, jnp.full((1, BLOCK), INT_MIN, jnp.int32))
    need = (top_k - count_ge(thr + 1)).astype(F32)

    acc_sc[...] = jnp.zeros_like(acc_sc)
    m_sc[...] = jnp.full_like(m_sc, NEG_BIG)
    l_sc[...] = jnp.zeros_like(l_sc)
    tri = jnp.where(lax.broadcasted_iota(jnp.int32, (kc, kc), 1)
                    < lax.broadcasted_iota(jnp.int32, (kc, kc), 0), 1.0, 0.0).astype(BF16)

    def p3(c, eq_seen):
        base = pl.multiple_of(c * kc, kc)
        key = keys_sc[pl.ds(base, kc), :]
        eq = key == thr
        eqf = jnp.where(eq, jnp.float32(1.0), jnp.float32(0.0))
        rank = jnp.dot(tri, eqf.astype(BF16), preferred_element_type=F32) + eq_seen
        sel = jnp.logical_or(key > thr, jnp.logical_and(eq, rank < need))
        sel = jnp.logical_and(sel, key != INT_MIN)
        ckc = ckv_ref[pl.ds(base, kc), :]
        ckt = ckvt_ref[c]
        off = pl.multiple_of(2 * kc - jnp.minimum(n * BLOCK - c * kc, 2 * kc), BLOCK)
        for h in range(N_HEADS):
            hs = slice(h * BLOCK, (h + 1) * BLOCK)
            lt = lax.dot_general(ckc, qabs_ref[hs, :], nt, preferred_element_type=F32)
            lt = jnp.where(sel, lt + bias_ref[h, pl.ds(off, kc), :], -jnp.inf)
            m_old = m_sc[h:h + 1, :]
            m_new = jnp.maximum(m_old, jnp.max(lt, axis=0, keepdims=True))
            alpha = jnp.exp(m_old - m_new)
            p = jnp.exp(lt - m_new)
            l_sc[h:h + 1, :] = alpha * l_sc[h:h + 1, :] + jnp.sum(p, axis=0, keepdims=True)
            m_sc[h:h + 1, :] = m_new
            acc_sc[hs, :] = alpha * acc_sc[hs, :] + jnp.dot(ckt, p.astype(BF16),
                                                            preferred_element_type=F32)
        return eq_seen + jnp.sum(eqf, axis=0, keepdims=True)

    lax.fori_loop(0, nch, p3, jnp.zeros((1, BLOCK), F32))

    for h in range(N_HEADS):
        hs = slice(h * BLOCK, (h + 1) * BLOCK)
        ot = (acc_sc[hs, :] * (1.0 / l_sc[h:h + 1, :])).astype(BF16)
        o_sc[h * HEAD_DIM:(h + 1) * HEAD_DIM, :] = jnp.dot(
            wuv_ref[h], ot, preferred_element_type=F32).astype(BF16)
    mt = jnp.dot(wout_ref[...], o_sc[...], preferred_element_type=F32)
    o_ref[...] = x_ref[...] + _rms(jnp.transpose(mt), post_ref[...])


def _dsa(x, pre_g, post_g, w_in, kv_g, ki_g, w_uk, w_uv, w_out, rel_bias):
    B, S, D = x.shape
    qdim = N_HEADS * HEAD_DIM
    qidim = IDX_HEADS * IDX_DIM
    tq = min(DSA_TQ, S)
    top_k = min(TOPK_MAX, S // 4)
    assert S % tq == 0 and tq % DSA_KC == 0 and KV_RANK == BLOCK
    c0, c1, c2, c3 = qdim, qdim + KV_RANK, qdim + KV_RANK + qidim, qdim + KV_RANK + qidim + IDX_DIM
    wq = w_in[:, :c0].astype(BF16)
    wqi = w_in[:, c1:c2].astype(BF16)
    wsm = jnp.concatenate([w_in[:, c0:c1], w_in[:, c2:c3], w_in[:, c3:]], axis=1)
    wsm = jnp.pad(wsm, ((0, 0), (0, DSA_SMALL - wsm.shape[1]))).astype(BF16)
    wuk = jnp.transpose(w_uk, (1, 2, 0)).astype(BF16)
    wuv = jnp.transpose(w_uv, (1, 2, 0)).astype(BF16)
    wout_t = jnp.transpose(w_out).astype(BF16)
    nblk = S // BLOCK

    proj = pl.pallas_call(
        functools.partial(_dsa_proj_body, tq=tq),
        out_shape=[
            jax.ShapeDtypeStruct((B, nblk, N_HEADS * BLOCK, KV_RANK), BF16),
            jax.ShapeDtypeStruct((B, S, KV_RANK), BF16),
            jax.ShapeDtypeStruct((B, S // DSA_KC, KV_RANK, DSA_KC), BF16),
            jax.ShapeDtypeStruct((B, S, qidim), BF16),
            jax.ShapeDtypeStruct((B, S, IDX_DIM), BF16),
            jax.ShapeDtypeStruct((B, IDX_HEADS, S), F32),
        ],
        grid=(B, S // tq),
        in_specs=[
            pl.BlockSpec((None, tq, D), lambda b, i: (b, i, 0)),
            _resident((1, D)),
            _resident((D, qdim)), _resident((D, qidim)), _resident((D, DSA_SMALL)),
            _resident((1, KV_RANK)), _resident((1, IDX_DIM)),
            _resident((N_HEADS, HEAD_DIM, KV_RANK)),
        ],
        out_specs=[
            pl.BlockSpec((None, tq // BLOCK, N_HEADS * BLOCK, KV_RANK), lambda b, i: (b, i, 0, 0)),
            pl.BlockSpec((None, tq, KV_RANK), lambda b, i: (b, i, 0)),
            pl.BlockSpec((None, tq // DSA_KC, KV_RANK, DSA_KC), lambda b, i: (b, i, 0, 0)),
            pl.BlockSpec((None, tq, qidim), lambda b, i: (b, i, 0)),
            pl.BlockSpec((None, tq, IDX_DIM), lambda b, i: (b, i, 0)),
            pl.BlockSpec((None, IDX_HEADS, tq), lambda b, i: (b, 0, i)),
        ],
        compiler_params=pltpu.CompilerParams(
            dimension_semantics=("parallel", "parallel"),
            vmem_limit_bytes=V7X_VMEM_LIMIT_BYTES),
        name="dsa_proj",
    )
    qabs, ckv, ckvt, qidx, kidx, widxt = proj(
        x, pre_g.reshape(1, D), wq, wqi, wsm, kv_g.reshape(1, KV_RANK), ki_g.reshape(1, IDX_DIM), wuk)

    dist = 2 * DSA_KC + jnp.arange(BLOCK)[None, :] - jnp.arange(BIAS_ROWS)[:, None]
    bias_t = jnp.transpose(rel_bias[_rel_bucket(dist)].astype(F32), (2, 0, 1))

    attn = pl.pallas_call(
        functools.partial(_dsa_attn_body, top_k=top_k),
        out_shape=jax.ShapeDtypeStruct((B, S, D), F32),
        grid=(B, nblk),
        in_specs=[
            pl.BlockSpec((None, BLOCK, D), lambda b, n: (b, n, 0)),
            pl.BlockSpec((None, None, N_HEADS * BLOCK, KV_RANK), lambda b, n: (b, n, 0, 0)),
            pl.BlockSpec((None, BLOCK, qidim), lambda b, n: (b, n, 0)),
            pl.BlockSpec((None, IDX_HEADS, BLOCK), lambda b, n: (b, 0, n)),
            pl.BlockSpec((None, S, IDX_DIM), lambda b, n: (b, 0, 0)),
            pl.BlockSpec((None, S, KV_RANK), lambda b, n: (b, 0, 0)),
            pl.BlockSpec((None, S // DSA_KC, KV_RANK, DSA_KC), lambda b, n: (b, 0, 0, 0)),
            _resident((N_HEADS, BIAS_ROWS, BLOCK)),
            _resident((N_HEADS, HEAD_DIM, KV_RANK)),
            _resident((D, qdim)),
            _resident((1, D)),
        ],
        out_specs=pl.BlockSpec((None, BLOCK, D), lambda b, n: (b, n, 0)),
        scratch_shapes=[
            pltpu.VMEM((S, BLOCK), jnp.int32),
            pltpu.VMEM((N_HEADS * KV_RANK, BLOCK), F32),
            pltpu.VMEM((N_HEADS, BLOCK), F32),
            pltpu.VMEM((N_HEADS, BLOCK), F32),
            pltpu.VMEM((qdim, BLOCK), BF16),
        ],
        compiler_params=pltpu.CompilerParams(
            dimension_semantics=("parallel", "parallel"),
            vmem_limit_bytes=V7X_VMEM_LIMIT_BYTES),
        name="dsa_attn",
    )
    return attn(x, qabs, qidx, widxt, kidx, ckv, ckvt, bias_t, wuv, wout_t, post_g.reshape(1, D))


def kernel(x, rel_bias, l0_norm_pre_mix, l0_norm_post_mix, l0_attn_w_in, l0_attn_sinks, l0_attn_w_out, l0_norm_pre_ffn, l0_norm_post_ffn, l0_ffn_w_up, l0_ffn_conv_w, l0_ffn_conv_b, l0_ffn_w_down, l1_norm_pre_mix, l1_norm_post_mix, l1_gmlp_w_in, l1_gmlp_ln_g, l1_gmlp_ln_b, l1_gmlp_w_s, l1_gmlp_b_s, l1_gmlp_w_out, l1_norm_pre_ffn, l1_norm_post_ffn, l1_ffn_w_up, l1_ffn_conv_w, l1_ffn_conv_b, l1_ffn_w_down, l2_norm_pre_mix, l2_norm_post_mix, l2_dsa_w_in, l2_dsa_kv_norm, l2_dsa_kidx_norm, l2_dsa_w_uk, l2_dsa_w_uv, l2_dsa_w_out, l2_norm_pre_ffn, l2_norm_post_ffn, l2_ffn_w_up, l2_ffn_conv_w, l2_ffn_conv_b, l2_ffn_w_down, l3_norm_pre_mix, l3_norm_post_mix, l3_attn_w_in, l3_attn_sinks, l3_attn_w_out, l3_norm_pre_ffn, l3_norm_post_ffn, l3_ffn_w_up, l3_ffn_conv_w, l3_ffn_conv_b, l3_ffn_w_down):
    x = _swa(x, l0_norm_pre_mix, l0_norm_post_mix, l0_attn_w_in, l0_attn_sinks, l0_attn_w_out, rel_bias)
    x = _ffn(x, l0_norm_pre_ffn, l0_norm_post_ffn, l0_ffn_w_up, l0_ffn_conv_w, l0_ffn_conv_b, l0_ffn_w_down)
    x = _gmlp(x, l1_norm_pre_mix, l1_norm_post_mix, l1_gmlp_w_in, l1_gmlp_ln_g, l1_gmlp_ln_b, l1_gmlp_w_s, l1_gmlp_b_s, l1_gmlp_w_out)
    x = _ffn(x, l1_norm_pre_ffn, l1_norm_post_ffn, l1_ffn_w_up, l1_ffn_conv_w, l1_ffn_conv_b, l1_ffn_w_down)
    x = _dsa(x, l2_norm_pre_mix, l2_norm_post_mix, l2_dsa_w_in, l2_dsa_kv_norm, l2_dsa_kidx_norm, l2_dsa_w_uk, l2_dsa_w_uv, l2_dsa_w_out, rel_bias)
    x = _ffn(x, l2_norm_pre_ffn, l2_norm_post_ffn, l2_ffn_w_up, l2_ffn_conv_w, l2_ffn_conv_b, l2_ffn_w_down)
    x = _swa(x, l3_norm_pre_mix, l3_norm_post_mix, l3_attn_w_in, l3_attn_sinks, l3_attn_w_out, rel_bias)
    x = _ffn(x, l3_norm_pre_ffn, l3_norm_post_ffn, l3_ffn_w_up, l3_ffn_conv_w, l3_ffn_conv_b, l3_ffn_w_down)
    return x
```

```python
import functools
import math

import jax
import jax.numpy as jnp
from jax import lax
from jax.experimental import pallas as pl
from jax.experimental.pallas import tpu as pltpu

F32 = jnp.float32
BF16 = jnp.bfloat16

EPS = 1e-6
N_HEADS = 16
HEAD_DIM = 64
N_KV_HEADS = 4
GQA_GROUP = N_HEADS // N_KV_HEADS
WINDOW = 128
BLOCK = 128
GMLP_CHUNK = 128
GMLP_GROUPS = 8
KV_RANK = 128
IDX_HEADS = 8
IDX_DIM = 64
TOPK_MAX = 256
REL_BUCKETS = 32
REL_MAX_DIST = 128
CONV_WIDTH = 3

V7X_LANES = 128
V7X_BF16_SUBLANES = 16
V7X_VMEM_LIMIT_BYTES = 56 * 1024 * 1024


def _rms(xf, g):
    ms = jnp.mean(xf * xf, axis=-1, keepdims=True)
    return xf * lax.rsqrt(ms + EPS) * g


def _resident(shape):
    nd = len(shape)
    return pl.BlockSpec(shape, lambda *_: (0,) * nd, pipeline_mode=pl.Buffered(1))


FFN_TM = 512
FFN_TN = 256
FFN_HALO = V7X_BF16_SUBLANES


def _ffn_body(x_ref, xh_ref, pre_ref, post_ref, wg_ref, wv_ref, cw_ref, cb_ref, wd_ref,
              o_ref, xn_sc, *, tm, nch, halo):
    i = pl.program_id(1)
    x = x_ref[...]
    pre = pre_ref[...]
    xn_sc[halo:, :] = _rms(x, pre).astype(BF16)
    xh = jnp.where(i > 0, _rms(xh_ref[...], pre), 0.0)
    xn_sc[:halo, :] = xh.astype(BF16)
    xn = xn_sc[...]

    def conv(h, w, b):
        h1 = pltpu.roll(h, 1, axis=0)
        h2 = pltpu.roll(h, 2, axis=0)
        y = h * w[2:3, :] + h1 * w[1:2, :] + h2 * w[0:1, :] + b
        return y[halo:, :]

    acc = jnp.zeros((tm, x.shape[-1]), F32)
    for j in range(nch):
        hg = jnp.dot(xn, wg_ref[j], preferred_element_type=F32)
        hv = jnp.dot(xn, wv_ref[j], preferred_element_type=F32)
        g = conv(hg, cw_ref[j], cb_ref[j])
        v = conv(hv, cw_ref[nch + j], cb_ref[nch + j])
        a = g * (1.0 / (1.0 + jnp.exp(-g))) * v
        acc = acc + jnp.dot(a.astype(BF16), wd_ref[j], preferred_element_type=F32)
    o_ref[...] = x + _rms(acc, post_ref[...])


def _ffn(x, pre_g, post_g, w_up, conv_w, conv_b, w_down):
    B, S, D = x.shape
    dff = w_down.shape[0]
    tm, tn, halo = min(FFN_TM, S), FFN_TN, FFN_HALO
    nch = dff // tn
    assert dff % tn == 0 and S % tm == 0 and tm % halo == 0
    wup = w_up.astype(BF16).reshape(D, 2, nch, tn)
    wg = jnp.transpose(wup[:, 0], (1, 0, 2))
    wv = jnp.transpose(wup[:, 1], (1, 0, 2))
    wd = w_down.astype(BF16).reshape(nch, tn, D)
    cw = jnp.transpose(conv_w.reshape(CONV_WIDTH, 2 * nch, tn), (1, 0, 2))
    cb = conv_b.reshape(2 * nch, 1, tn)
    body = functools.partial(_ffn_body, tm=tm, nch=nch, halo=halo)
    return pl.pallas_call(
        body,
        out_shape=jax.ShapeDtypeStruct((B, S, D), F32),
        grid=(B, S // tm),
        in_specs=[
            pl.BlockSpec((None, tm, D), lambda b, i: (b, i, 0)),
            pl.BlockSpec((None, halo, D), lambda b, i: (b, jnp.maximum(i * (tm // halo) - 1, 0), 0)),
            _resident((1, D)), _resident((1, D)),
            _resident((nch, D, tn)), _resident((nch, D, tn)),
            _resident((2 * nch, CONV_WIDTH, tn)), _resident((2 * nch, 1, tn)),
            _resident((nch, tn, D)),
        ],
        out_specs=pl.BlockSpec((None, tm, D), lambda b, i: (b, i, 0)),
        scratch_shapes=[pltpu.VMEM((halo + tm, D), BF16)],
        compiler_params=pltpu.CompilerParams(
            dimension_semantics=("parallel", "parallel"),
            vmem_limit_bytes=V7X_VMEM_LIMIT_BYTES),
        name="ffn",
    )(x, x, pre_g.reshape(1, D), post_g.reshape(1, D), wg, wv, cw, cb, wd)


def _rel_bucket(dist):
    max_exact = REL_BUCKETS // 2
    d = jnp.maximum(dist, 0)
    df = jnp.maximum(d, 1).astype(F32)
    large = max_exact + (jnp.log(df / max_exact) / math.log(REL_MAX_DIST / max_exact)
                         * (REL_BUCKETS - max_exact)).astype(jnp.int32)
    large = jnp.minimum(large, REL_BUCKETS - 1)
    return jnp.where(d < max_exact, d, large)


def _bias_lookup(rel_bias, dist):
    onehot = _rel_bucket(dist)[None, ..., None] == jnp.arange(REL_BUCKETS)
    table = jnp.transpose(rel_bias.astype(F32)).reshape((rel_bias.shape[1],) + (1,) * dist.ndim + (REL_BUCKETS,))
    return jnp.sum(jnp.where(onehot, table, 0.0), axis=-1)


SWA_TQ = 256


def _swa_body(x_ref, pre_ref, post_ref, win_ref, sink_ref, bias_ref, wout_ref, o_ref,
              kv_sc, o_sc, *, tq):
    i = pl.program_id(1)
    qdim = N_HEADS * HEAD_DIM
    kdim = N_KV_HEADS * HEAD_DIM

    @pl.when(i == 0)
    def _():
        kv_sc[0] = jnp.zeros(kv_sc.shape[1:], BF16)

    x = x_ref[...]
    xn = _rms(x, pre_ref[...]).astype(BF16)
    qkv = jnp.dot(xn, win_ref[...], preferred_element_type=F32)
    q = (qkv[:, :qdim] * (HEAD_DIM ** -0.5)).astype(BF16)
    kv = qkv[:, qdim:].astype(BF16)
    col = lax.broadcasted_iota(jnp.int32, (BLOCK, 2 * BLOCK), 1)
    first_masked = jnp.where(i == 0, BLOCK, 0)
    for r in range(tq // BLOCK):
        rows = slice(r * BLOCK, (r + 1) * BLOCK)
        kv_prev = kv_sc[i % 2] if r == 0 else kv[(r - 1) * BLOCK:r * BLOCK]
        kvw = jnp.concatenate([kv_prev, kv[rows]], axis=0)
        for kvh in range(N_KV_HEADS):
            kh = kvw[:, kvh * HEAD_DIM:(kvh + 1) * HEAD_DIM]
            vh = kvw[:, kdim + kvh * HEAD_DIM:kdim + (kvh + 1) * HEAD_DIM]
            for g in range(GQA_GROUP):
                h = kvh * GQA_GROUP + g
                qh = q[rows, h * HEAD_DIM:(h + 1) * HEAD_DIM]
                s = lax.dot_general(qh, kh, (((1,), (1,)), ((), ())),
                                    preferred_element_type=F32) + bias_ref[h]
                if r == 0:
                    s = jnp.where(col < first_masked, -jnp.inf, s)
                sink = sink_ref[h]
                m = jnp.maximum(jnp.max(s, axis=-1, keepdims=True), sink)
                p = jnp.exp(s - m)
                den = jnp.sum(p, axis=-1, keepdims=True) + jnp.exp(sink - m)
                oh = jnp.dot(p.astype(BF16), vh, preferred_element_type=F32) * (1.0 / den)
                o_sc[rows, h * HEAD_DIM:(h + 1) * HEAD_DIM] = oh.astype(BF16)
    kv_sc[(i + 1) % 2] = kv[tq - BLOCK:]
    mo = jnp.dot(o_sc[...], wout_ref[...], preferred_element_type=F32)
    o_ref[...] = x + _rms(mo, post_ref[...])


def _swa(x, pre_g, post_g, w_in, sinks, w_out, rel_bias):
    B, S, D = x.shape
    tq = min(SWA_TQ, S)
    qdim, kdim = N_HEADS * HEAD_DIM, N_KV_HEADS * HEAD_DIM
    assert S % tq == 0 and tq % BLOCK == 0
    dist = (jnp.arange(BLOCK)[:, None] + BLOCK) - jnp.arange(2 * BLOCK)[None, :]
    band = (dist >= 0) & (dist < WINDOW)
    bias = jnp.where(band[None], _bias_lookup(rel_bias, dist), -jnp.inf)
    body = functools.partial(_swa_body, tq=tq)
    return pl.pallas_call(
        body,
        out_shape=jax.ShapeDtypeStruct((B, S, D), F32),
        grid=(B, S // tq),
        in_specs=[
            pl.BlockSpec((None, tq, D), lambda b, i: (b, i, 0)),
            _resident((1, D)), _resident((1, D)),
            _resident((D, qdim + 2 * kdim)),
            pl.BlockSpec(memory_space=pltpu.SMEM),
            _resident((N_HEADS, BLOCK, 2 * BLOCK)),
            _resident((qdim, D)),
        ],
        out_specs=pl.BlockSpec((None, tq, D), lambda b, i: (b, i, 0)),
        scratch_shapes=[pltpu.VMEM((2, BLOCK, 2 * kdim), BF16), pltpu.VMEM((tq, qdim), BF16)],
        compiler_params=pltpu.CompilerParams(
            dimension_semantics=("arbitrary", "arbitrary"),
            vmem_limit_bytes=V7X_VMEM_LIMIT_BYTES),
        name="swa",
    )(x, pre_g.reshape(1, D), post_g.reshape(1, D), w_in.astype(BF16), sinks.astype(F32),
      bias, w_out.astype(BF16))


GMLP_TQ = 256


def _gmlp_body(x_ref, pre_ref, post_ref, win_ref, lng_ref, lnb_ref, ws_ref, bs_ref, wout_ref,
               o_ref, gated_sc, *, tq):
    x = x_ref[...]
    width = wout_ref.shape[0]
    gdim = width // GMLP_GROUPS
    xn = _rms(x, pre_ref[...]).astype(BF16)
    h = jax.nn.gelu(jnp.dot(xn, win_ref[...], preferred_element_type=F32))
    u, v = h[:, :width], h[:, width:]
    mu = jnp.mean(v, axis=-1, keepdims=True)
    var = jnp.mean(jnp.square(v - mu), axis=-1, keepdims=True)
    vn = ((v - mu) * lax.rsqrt(var + EPS) * lng_ref[...] + lnb_ref[...]).astype(BF16)
    for c in range(tq // GMLP_CHUNK):
        rows = slice(c * GMLP_CHUNK, (c + 1) * GMLP_CHUNK)
        for g in range(GMLP_GROUPS):
            cols = slice(g * gdim, (g + 1) * gdim)
            mixed = jnp.dot(ws_ref[g], vn[rows, cols], preferred_element_type=F32) + bs_ref[g]
            gated_sc[rows, cols] = (u[rows, cols] * mixed).astype(BF16)
    mo = jnp.dot(gated_sc[...], wout_ref[...], preferred_element_type=F32)
    o_ref[...] = x + _rms(mo, post_ref[...])


def _gmlp(x, pre_g, post_g, w_in, ln_g, ln_b, w_s, b_s, w_out):
    B, S, D = x.shape
    width = w_out.shape[0]
    gdim = width // GMLP_GROUPS
    tq = min(GMLP_TQ, S)
    assert S % tq == 0 and tq % GMLP_CHUNK == 0 and gdim == V7X_LANES
    causal = jnp.tril(jnp.ones((GMLP_CHUNK, GMLP_CHUNK), dtype=bool))
    ws = jnp.where(causal, w_s, jnp.zeros_like(w_s)).astype(BF16)
    bs = jnp.broadcast_to(b_s[:, :, None], (GMLP_GROUPS, GMLP_CHUNK, gdim)).astype(F32)
    body = functools.partial(_gmlp_body, tq=tq)
    return pl.pallas_call(
        body,
        out_shape=jax.ShapeDtypeStruct((B, S, D), F32),
        grid=(B, S // tq),
        in_specs=[
            pl.BlockSpec((None, tq, D), lambda b, i: (b, i, 0)),
            _resident((1, D)), _resident((1, D)),
            _resident((D, 2 * width)),
            _resident((1, width)), _resident((1, width)),
            _resident((GMLP_GROUPS, GMLP_CHUNK, GMLP_CHUNK)),
            _resident((GMLP_GROUPS, GMLP_CHUNK, gdim)),
            _resident((width, D)),
        ],
        out_specs=pl.BlockSpec((None, tq, D), lambda b, i: (b, i, 0)),
        scratch_shapes=[pltpu.VMEM((tq, width), BF16)],
        compiler_params=pltpu.CompilerParams(
            dimension_semantics=("parallel", "parallel"),
            vmem_limit_bytes=V7X_VMEM_LIMIT_BYTES),
        name="gmlp",
    )(x, pre_g.reshape(1, D), post_g.reshape(1, D), w_in.astype(BF16),
      ln_g.reshape(1, width), ln_b.reshape(1, width), ws, bs, w_out.astype(BF16))


DSA_TQ = 256
DSA_KC = 256
DSA_SMALL = 256
DSA_AUG = 16
INT_MIN = -2 ** 31
NEG_BIG = -1e30
LOG2E = math.log2(math.e)
COUNT_LANES = 4


def _dsa_proj_body(x_ref, pre_ref, wq_ref, wqi_ref, wsm_ref, kvg_ref, kig_ref, wuk_ref,
                   qabs_ref, ckv_ref, ckvt_ref, qidx_ref, kidx_ref, widxt_ref, *, tq):
    xn = _rms(x_ref[...], pre_ref[...]).astype(BF16)
    q = jnp.dot(xn, wq_ref[...], preferred_element_type=F32).astype(BF16)
    for h in range(N_HEADS):
        qa = jnp.dot(q[:, h * HEAD_DIM:(h + 1) * HEAD_DIM], wuk_ref[h],
                     preferred_element_type=F32) * (HEAD_DIM ** -0.5 * LOG2E)
        qa = qa.astype(BF16)
        for blk in range(tq // BLOCK):
            qabs_ref[blk, h * BLOCK:(h + 1) * BLOCK, :] = qa[blk * BLOCK:(blk + 1) * BLOCK]
    qi = jnp.dot(xn, wqi_ref[...], preferred_element_type=F32).astype(BF16)
    for h in range(IDX_HEADS):
        for blk in range(tq // BLOCK):
            qidx_ref[blk, h * BLOCK:(h + 1) * BLOCK, :] = qi[blk * BLOCK:(blk + 1) * BLOCK,
                                                              h * IDX_DIM:(h + 1) * IDX_DIM]
    sm = jnp.dot(xn, wsm_ref[...], preferred_element_type=F32)
    ckv = _rms(sm[:, :KV_RANK], kvg_ref[...])
    ckv_ref[...] = ckv.astype(BF16)
    ones_row = jnp.where(lax.broadcasted_iota(jnp.int32, (DSA_AUG, DSA_KC), 0) == 0,
                         jnp.float32(1.0), jnp.float32(0.0)).astype(BF16)
    for cc in range(tq // DSA_KC):
        ckvt_ref[cc, :KV_RANK, :] = jnp.transpose(ckv[cc * DSA_KC:(cc + 1) * DSA_KC]).astype(BF16)
        ckvt_ref[cc, KV_RANK:, :] = ones_row
    kidx_ref[...] = _rms(sm[:, KV_RANK:KV_RANK + IDX_DIM], kig_ref[...]).astype(BF16)
    tail = jnp.transpose(sm[:, KV_RANK:2 * KV_RANK])
    widxt_ref[...] = tail[IDX_DIM:IDX_DIM + IDX_HEADS, :] * (IDX_HEADS ** -0.5 * IDX_DIM ** -0.5)


def _dsa_attn_body(x_ref, qabs_ref, qidx_ref, widxt_ref, kidx_ref, ckv_ref, ckvt_ref, bias_ref,
                   tri_ref, wuv_ref, wout_ref, post_ref, o_ref,
                   keys_sc, acc_sc, m_sc, o_sc, am_sc, p_sc, alpha_sc, *, top_k):
    n = pl.program_id(1)
    kc = DSA_KC
    nch = (n + 2) // 2
    lane = lax.broadcasted_iota(jnp.int32, (kc, BLOCK), 1)
    row = lax.broadcasted_iota(jnp.int32, (kc, BLOCK), 0)
    qpos = n * BLOCK + lane
    nt = (((1,), (1,)), ((), ()))

    def p1(c, carry):
        base = pl.multiple_of(c * kc, kc)
        xt = lax.dot_general(kidx_ref[pl.ds(base, kc), :], qidx_ref[...], nt,
                             preferred_element_type=F32)
        acc = jnp.zeros((kc, BLOCK), F32)
        for h in range(IDX_HEADS):
            acc = acc + widxt_ref[h:h + 1, :] * jnp.maximum(xt[:, h * BLOCK:(h + 1) * BLOCK], 0.0)
        bits = pltpu.bitcast(acc, jnp.int32)
        key = bits ^ ((bits >> 31) & 0x7FFFFFFF)
        keys_sc[pl.ds(base, kc), :] = jnp.where(c * kc + row <= qpos, key, INT_MIN)
        return carry

    lax.fori_loop(0, nch, p1, 0)

    def count_ge(cand):
        def body(c, cnts):
            k = keys_sc[pl.ds(pl.multiple_of(c * kc, kc), kc), :]
            cnts = list(cnts)
            for j in range(kc // 8):
                a = j % COUNT_LANES
                cnts[a] = jnp.where(k[j * 8:(j + 1) * 8, :] >= cand, cnts[a] + 1, cnts[a])
            return tuple(cnts)
        zero = jnp.zeros((8, BLOCK), jnp.int32)
        cnts = lax.fori_loop(0, nch, body, (zero,) * COUNT_LANES)
        total = functools.reduce(lambda u, v: u + v, cnts)
        return jnp.sum(total, axis=0, keepdims=True, dtype=jnp.int32)

    def bisect(it, prefix):
        cand = prefix ^ jnp.left_shift(jnp.int32(1), 31 - it)
        return jnp.where(count_ge(cand) >= top_k, cand, prefix)

    thr = lax.fori_loop(0, 32, bisect, jnp.full((1, BLOCK), INT_MIN, jnp.int32))
    need = (top_k - count_ge(thr + 1)).astype(F32)

    acc_sc[...] = jnp.zeros_like(acc_sc)
    m_sc[...] = jnp.full_like(m_sc, NEG_BIG)

    def probs(c, slot, eq_seen):
        base = pl.multiple_of(c * kc, kc)
        key = keys_sc[pl.ds(base, kc), :]
        eq = key == thr
        eqf = jnp.where(eq, jnp.float32(1.0), jnp.float32(0.0))
        rank = jnp.dot(tri_ref[...], eqf.astype(BF16), preferred_element_type=F32) + eq_seen
        sel = jnp.logical_or(key > thr, jnp.logical_and(eq, rank < need))
        sel = jnp.logical_and(sel, key != INT_MIN)
        am_sc[...] = jnp.where(sel, jnp.float32(0.0), jnp.float32(-jnp.inf))
        ckc = ckv_ref[pl.ds(base, kc), :]
        off = pl.multiple_of(2 * kc - jnp.minimum(n * BLOCK - c * kc, 2 * kc), BLOCK)
        for h in range(N_HEADS):
            lt = lax.dot_general(ckc, qabs_ref[h * BLOCK:(h + 1) * BLOCK, :], nt,
                                 preferred_element_type=F32)
            lt = lt + (am_sc[...] + bias_ref[h, pl.ds(off, kc), :])
            m_old = m_sc[h]
            m_new = jnp.maximum(m_old, jnp.max(lt, axis=0, keepdims=True))
            m_sc[h] = m_new
            alpha_sc[slot, h] = jnp.exp2(m_old - m_new)
            p_sc[slot, h] = jnp.exp2(lt - m_new).astype(BF16)
        return eq_seen + jnp.sum(eqf, axis=0, keepdims=True)

    def accumulate(c, slot):
        ckt = ckvt_ref[c]
        for h in range(N_HEADS):
            acc_sc[h] = alpha_sc[slot, h] * acc_sc[h] + jnp.dot(
                ckt, p_sc[slot, h], preferred_element_type=F32)

    def p3(c, eq_seen):
        eq_seen = probs(c, c % 2, eq_seen)
        accumulate(c - 1, (c - 1) % 2)
        return eq_seen

    lax.fori_loop(1, nch, p3, probs(0, 0, jnp.zeros((1, BLOCK), F32)))
    accumulate(nch - 1, (nch - 1) % 2)

    for h in range(N_HEADS):
        a = acc_sc[h]
        ot = (a[:KV_RANK] * (1.0 / a[KV_RANK:KV_RANK + 1])).astype(BF16)
        o_sc[h * HEAD_DIM:(h + 1) * HEAD_DIM, :] = jnp.dot(
            wuv_ref[h], ot, preferred_element_type=F32).astype(BF16)
    mt = jnp.dot(wout_ref[...], o_sc[...], preferred_element_type=F32)
    o_ref[...] = x_ref[...] + _rms(jnp.transpose(mt), post_ref[...])


def _dsa(x, pre_g, post_g, w_in, kv_g, ki_g, w_uk, w_uv, w_out, rel_bias):
    B, S, D = x.shape
    qdim = N_HEADS * HEAD_DIM
    qidim = IDX_HEADS * IDX_DIM
    tq = min(DSA_TQ, S)
    top_k = min(TOPK_MAX, S // 4)
    assert S % tq == 0 and tq % DSA_KC == 0 and KV_RANK == BLOCK and DSA_KC == 2 * BLOCK
    c0, c1, c2, c3 = qdim, qdim + KV_RANK, qdim + KV_RANK + qidim, qdim + KV_RANK + qidim + IDX_DIM
    wq = w_in[:, :c0].astype(BF16)
    wqi = w_in[:, c1:c2].astype(BF16)
    wsm = jnp.concatenate([w_in[:, c0:c1], w_in[:, c2:c3], w_in[:, c3:]], axis=1)
    wsm = jnp.pad(wsm, ((0, 0), (0, DSA_SMALL - wsm.shape[1]))).astype(BF16)
    wuk = jnp.transpose(w_uk, (1, 2, 0)).astype(BF16)
    wuv = jnp.transpose(w_uv, (1, 2, 0)).astype(BF16)
    wout_t = jnp.transpose(w_out).astype(BF16)
    nblk = S // BLOCK
    raug = KV_RANK + DSA_AUG

    proj = pl.pallas_call(
        functools.partial(_dsa_proj_body, tq=tq),
        out_shape=[
            jax.ShapeDtypeStruct((B, nblk, N_HEADS * BLOCK, KV_RANK), BF16),
            jax.ShapeDtypeStruct((B, S, KV_RANK), BF16),
            jax.ShapeDtypeStruct((B, S // DSA_KC, raug, DSA_KC), BF16),
            jax.ShapeDtypeStruct((B, nblk, IDX_HEADS * BLOCK, IDX_DIM), BF16),
            jax.ShapeDtypeStruct((B, S, IDX_DIM), BF16),
            jax.ShapeDtypeStruct((B, IDX_HEADS, S), F32),
        ],
        grid=(B, S // tq),
        in_specs=[
            pl.BlockSpec((None, tq, D), lambda b, i: (b, i, 0)),
            _resident((1, D)),
            _resident((D, qdim)), _resident((D, qidim)), _resident((D, DSA_SMALL)),
            _resident((1, KV_RANK)), _resident((1, IDX_DIM)),
            _resident((N_HEADS, HEAD_DIM, KV_RANK)),
        ],
        out_specs=[
            pl.BlockSpec((None, tq // BLOCK, N_HEADS * BLOCK, KV_RANK), lambda b, i: (b, i, 0, 0)),
            pl.BlockSpec((None, tq, KV_RANK), lambda b, i: (b, i, 0)),
            pl.BlockSpec((None, tq // DSA_KC, raug, DSA_KC), lambda b, i: (b, i, 0, 0)),
            pl.BlockSpec((None, tq // BLOCK, IDX_HEADS * BLOCK, IDX_DIM), lambda b, i: (b, i, 0, 0)),
            pl.BlockSpec((None, tq, IDX_DIM), lambda b, i: (b, i, 0)),
            pl.BlockSpec((None, IDX_HEADS, tq), lambda b, i: (b, 0, i)),
        ],
        compiler_params=pltpu.CompilerParams(
            dimension_semantics=("parallel", "parallel"),
            vmem_limit_bytes=V7X_VMEM_LIMIT_BYTES),
        name="dsa_proj",
    )
    qabs, ckv, ckvt, qidx, kidx, widxt = proj(
        x, pre_g.reshape(1, D), wq, wqi, wsm, kv_g.reshape(1, KV_RANK), ki_g.reshape(1, IDX_DIM), wuk)

    dist = 2 * DSA_KC + jnp.arange(BLOCK)[None, :] - jnp.arange(3 * DSA_KC)[:, None]
    far = rel_bias[REL_BUCKETS - 1].astype(F32)
    bias_t = (_bias_lookup(rel_bias, dist) - far[:, None, None]) * LOG2E
    tri = jnp.tril(jnp.ones((DSA_KC, DSA_KC), F32), -1).astype(BF16)

    attn = pl.pallas_call(
        functools.partial(_dsa_attn_body, top_k=top_k),
        out_shape=jax.ShapeDtypeStruct((B, S, D), F32),
        grid=(B, nblk),
        in_specs=[
            pl.BlockSpec((None, BLOCK, D), lambda b, n: (b, n, 0)),
            pl.BlockSpec((None, None, N_HEADS * BLOCK, KV_RANK), lambda b, n: (b, n, 0, 0)),
            pl.BlockSpec((None, None, IDX_HEADS * BLOCK, IDX_DIM), lambda b, n: (b, n, 0, 0)),
            pl.BlockSpec((None, IDX_HEADS, BLOCK), lambda b, n: (b, 0, n)),
            pl.BlockSpec((None, S, IDX_DIM), lambda b, n: (b, 0, 0)),
            pl.BlockSpec((None, S, KV_RANK), lambda b, n: (b, 0, 0)),
            pl.BlockSpec((None, S // DSA_KC, raug, DSA_KC), lambda b, n: (b, 0, 0, 0)),
            _resident((N_HEADS, 3 * DSA_KC, BLOCK)),
            _resident((DSA_KC, DSA_KC)),
            _resident((N_HEADS, HEAD_DIM, KV_RANK)),
            _resident((D, qdim)),
            _resident((1, D)),
        ],
        out_specs=pl.BlockSpec((None, BLOCK, D), lambda b, n: (b, n, 0)),
        scratch_shapes=[
            pltpu.VMEM((S, BLOCK), jnp.int32),
            pltpu.VMEM((N_HEADS, raug, BLOCK), F32),
            pltpu.VMEM((N_HEADS, 1, BLOCK), F32),
            pltpu.VMEM((qdim, BLOCK), BF16),
            pltpu.VMEM((DSA_KC, BLOCK), F32),
            pltpu.VMEM((2, N_HEADS, DSA_KC, BLOCK), BF16),
            pltpu.VMEM((2, N_HEADS, 1, BLOCK), F32),
        ],
        compiler_params=pltpu.CompilerParams(
            dimension_semantics=("parallel", "parallel"),
            vmem_limit_bytes=V7X_VMEM_LIMIT_BYTES),
        name="dsa_attn",
    )
    return attn(x, qabs, qidx, widxt, kidx, ckv, ckvt, bias_t, tri, wuv, wout_t, post_g.reshape(1, D))


def kernel(x, rel_bias, l0_norm_pre_mix, l0_norm_post_mix, l0_attn_w_in, l0_attn_sinks, l0_attn_w_out, l0_norm_pre_ffn, l0_norm_post_ffn, l0_ffn_w_up, l0_ffn_conv_w, l0_ffn_conv_b, l0_ffn_w_down, l1_norm_pre_mix, l1_norm_post_mix, l1_gmlp_w_in, l1_gmlp_ln_g, l1_gmlp_ln_b, l1_gmlp_w_s, l1_gmlp_b_s, l1_gmlp_w_out, l1_norm_pre_ffn, l1_norm_post_ffn, l1_ffn_w_up, l1_ffn_conv_w, l1_ffn_conv_b, l1_ffn_w_down, l2_norm_pre_mix, l2_norm_post_mix, l2_dsa_w_in, l2_dsa_kv_norm, l2_dsa_kidx_norm, l2_dsa_w_uk, l2_dsa_w_uv, l2_dsa_w_out, l2_norm_pre_ffn, l2_norm_post_ffn, l2_ffn_w_up, l2_ffn_conv_w, l2_ffn_conv_b, l2_ffn_w_down, l3_norm_pre_mix, l3_norm_post_mix, l3_attn_w_in, l3_attn_sinks, l3_attn_w_out, l3_norm_pre_ffn, l3_norm_post_ffn, l3_ffn_w_up, l3_ffn_conv_w, l3_ffn_conv_b, l3_ffn_w_down):
    x = _swa(x, l0_norm_pre_mix, l0_norm_post_mix, l0_attn_w_in, l0_attn_sinks, l0_attn_w_out, rel_bias)
    x = _ffn(x, l0_norm_pre_ffn, l0_norm_post_ffn, l0_ffn_w_up, l0_ffn_conv_w, l0_ffn_conv_b, l0_ffn_w_down)
    x = _gmlp(x, l1_norm_pre_mix, l1_norm_post_mix, l1_gmlp_w_in, l1_gmlp_ln_g, l1_gmlp_ln_b, l1_gmlp_w_s, l1_gmlp_b_s, l1_gmlp_w_out)
    x = _ffn(x, l1_norm_pre_ffn, l1_norm_post_ffn, l1_ffn_w_up, l1_ffn_conv_w, l1_ffn_conv_b, l1_ffn_w_down)
    x = _dsa(x, l2_norm_pre_mix, l2_norm_post_mix, l2_dsa_w_in, l2_dsa_kv_norm, l2_dsa_kidx_norm, l2_dsa_w_uk, l2_dsa_w_uv, l2_dsa_w_out, rel_bias)
    x = _ffn(x, l2_norm_pre_ffn, l2_norm_post_ffn, l2_ffn_w_up, l2_ffn_conv_w, l2_ffn_conv_b, l2_ffn_w_down)
    x = _swa(x, l3_norm_pre_mix, l3_norm_post_mix, l3_attn_w_in, l3_attn_sinks, l3_attn_w_out, rel_bias)
    x = _ffn(x, l3_norm_pre_ffn, l3_norm_post_ffn, l3_ffn_w_up, l3_ffn_conv_w, l3_ffn_conv_b, l3_ffn_w_down)
    return x
```

```python
import functools
import math

import jax
import jax.numpy as jnp
from jax import lax
from jax.experimental import pallas as pl
from jax.experimental.pallas import tpu as pltpu

F32 = jnp.float32
BF16 = jnp.bfloat16

EPS = 1e-6
N_HEADS = 16
HEAD_DIM = 64
N_KV_HEADS = 4
GQA_GROUP = N_HEADS // N_KV_HEADS
WINDOW = 128
BLOCK = 128
GMLP_CHUNK = 128
GMLP_GROUPS = 8
KV_RANK = 128
IDX_HEADS = 8
IDX_DIM = 64
TOPK_MAX = 256
REL_BUCKETS = 32
REL_MAX_DIST = 128
CONV_WIDTH = 3

V7X_LANES = 128
V7X_BF16_SUBLANES = 16
V7X_VMEM_LIMIT_BYTES = 56 * 1024 * 1024

LOG2E = math.log2(math.e)


def _rms(xf, g):
    ms = jnp.mean(xf * xf, axis=-1, keepdims=True)
    return xf * lax.rsqrt(ms + EPS) * g


def _resident(shape):
    nd = len(shape)
    return pl.BlockSpec(shape, lambda *_: (0,) * nd, pipeline_mode=pl.Buffered(1))


FFN_TM = 512
FFN_TN = 256
FFN_HALO = V7X_BF16_SUBLANES


def _ffn_body(x_ref, xh_ref, pre_ref, post_ref, wg_ref, wv_ref, cw_ref, cb_ref, wd_ref,
              o_ref, xn_sc, h_sc, a_sc, acc_sc, *, tm, nch, halo):
    i = pl.program_id(1)
    pre = pre_ref[...]
    xn_sc[halo:, :] = _rms(x_ref[...], pre).astype(BF16)
    xh = jnp.where(i > 0, _rms(xh_ref[...], pre), 0.0)
    xn_sc[:halo, :] = xh.astype(BF16)
    acc_sc[...] = jnp.zeros_like(acc_sc)

    def up(j, slot):
        xn = xn_sc[...]
        h_sc[slot, 0] = jnp.dot(xn, wg_ref[j], preferred_element_type=F32)
        h_sc[slot, 1] = jnp.dot(xn, wv_ref[j], preferred_element_type=F32)

    def conv(slot, part, w, b):
        h0 = h_sc[slot, part, halo:, :]
        h1 = h_sc[slot, part, halo - 1:halo - 1 + tm, :]
        h2 = h_sc[slot, part, halo - 2:halo - 2 + tm, :]
        return h0 * w[2:3, :] + h1 * w[1:2, :] + h2 * w[0:1, :] + b

    def act(j, slot):
        g = conv(slot, 0, cw_ref[j], cb_ref[j])
        v = conv(slot, 1, cw_ref[nch + j], cb_ref[nch + j])
        a_sc[slot] = (g * (1.0 / (1.0 + jnp.exp(-g))) * v).astype(BF16)

    def down(j, slot):
        acc_sc[...] += jnp.dot(a_sc[slot], wd_ref[j], preferred_element_type=F32)

    for j in range(nch + 2):
        if j >= 2:
            down(j - 2, j % 2)
        if 1 <= j <= nch:
            act(j - 1, (j - 1) % 2)
        if j < nch:
            up(j, j % 2)
    o_ref[...] = x_ref[...] + _rms(acc_sc[...], post_ref[...])


def _ffn(x, pre_g, post_g, w_up, conv_w, conv_b, w_down):
    B, S, D = x.shape
    dff = w_down.shape[0]
    tm, tn, halo = min(FFN_TM, S), FFN_TN, FFN_HALO
    nch = dff // tn
    assert dff % tn == 0 and S % tm == 0 and tm % halo == 0
    wup = w_up.astype(BF16).reshape(D, 2, nch, tn)
    wg = jnp.transpose(wup[:, 0], (1, 0, 2))
    wv = jnp.transpose(wup[:, 1], (1, 0, 2))
    wd = w_down.astype(BF16).reshape(nch, tn, D)
    cw = jnp.transpose(conv_w.reshape(CONV_WIDTH, 2 * nch, tn), (1, 0, 2))
    cb = conv_b.reshape(2 * nch, 1, tn)
    body = functools.partial(_ffn_body, tm=tm, nch=nch, halo=halo)
    return pl.pallas_call(
        body,
        out_shape=jax.ShapeDtypeStruct((B, S, D), F32),
        grid=(B, S // tm),
        in_specs=[
            pl.BlockSpec((None, tm, D), lambda b, i: (b, i, 0)),
            pl.BlockSpec((None, halo, D), lambda b, i: (b, jnp.maximum(i * (tm // halo) - 1, 0), 0)),
            _resident((1, D)), _resident((1, D)),
            _resident((nch, D, tn)), _resident((nch, D, tn)),
            _resident((2 * nch, CONV_WIDTH, tn)), _resident((2 * nch, 1, tn)),
            _resident((nch, tn, D)),
        ],
        out_specs=pl.BlockSpec((None, tm, D), lambda b, i: (b, i, 0)),
        scratch_shapes=[
            pltpu.VMEM((halo + tm, D), BF16),
            pltpu.VMEM((2, 2, halo + tm, tn), F32),
            pltpu.VMEM((2, tm, tn), BF16),
            pltpu.VMEM((tm, D), F32),
        ],
        compiler_params=pltpu.CompilerParams(
            dimension_semantics=("parallel", "parallel"),
            vmem_limit_bytes=V7X_VMEM_LIMIT_BYTES,
            ),
        name="ffn",
    )(x, x, pre_g.reshape(1, D), post_g.reshape(1, D), wg, wv, cw, cb, wd)


def _rel_bucket(dist):
    max_exact = REL_BUCKETS // 2
    d = jnp.maximum(dist, 0)
    df = jnp.maximum(d, 1).astype(F32)
    large = max_exact + (jnp.log(df / max_exact) / math.log(REL_MAX_DIST / max_exact)
                         * (REL_BUCKETS - max_exact)).astype(jnp.int32)
    large = jnp.minimum(large, REL_BUCKETS - 1)
    return jnp.where(d < max_exact, d, large)


def _bias_lookup(rel_bias, dist):
    onehot = _rel_bucket(dist)[None, ..., None] == jnp.arange(REL_BUCKETS)
    table = jnp.transpose(rel_bias.astype(F32)).reshape((rel_bias.shape[1],) + (1,) * dist.ndim + (REL_BUCKETS,))
    return jnp.sum(jnp.where(onehot, table, 0.0), axis=-1)


SWA_TQ = 256


def _swa_body(x_ref, pre_ref, post_ref, win_ref, sink_ref, bias_ref, wout_ref, o_ref,
              kv_sc, o_sc, s_sc, p_sc, *, tq):
    i = pl.program_id(1)
    qdim = N_HEADS * HEAD_DIM
    kdim = N_KV_HEADS * HEAD_DIM

    @pl.when(i == 0)
    def _():
        kv_sc[0] = jnp.zeros(kv_sc.shape[1:], BF16)

    x = x_ref[...]
    xn = _rms(x, pre_ref[...]).astype(BF16)
    qkv = jnp.dot(xn, win_ref[...], preferred_element_type=F32)
    q = (qkv[:, :qdim] * (HEAD_DIM ** -0.5 * LOG2E)).astype(BF16)
    kv = qkv[:, qdim:].astype(BF16)
    col = lax.broadcasted_iota(jnp.int32, (BLOCK, 2 * BLOCK), 1)
    first_masked = jnp.where(i == 0, BLOCK, 0)
    for r in range(tq // BLOCK):
        rows = slice(r * BLOCK, (r + 1) * BLOCK)
        kv_prev = kv_sc[i % 2] if r == 0 else kv[(r - 1) * BLOCK:r * BLOCK]
        kvw = jnp.concatenate([kv_prev, kv[rows]], axis=0)
        for h in range(N_HEADS):
            kvh = h // GQA_GROUP
            s = lax.dot_general(q[rows, h * HEAD_DIM:(h + 1) * HEAD_DIM],
                                kvw[:, kvh * HEAD_DIM:(kvh + 1) * HEAD_DIM],
                                (((1,), (1,)), ((), ())), preferred_element_type=F32) + bias_ref[h]
            if r == 0:
                s = jnp.where(col < first_masked, -jnp.inf, s)
            s_sc[h] = s
        for h in range(N_HEADS):
            s = s_sc[h]
            sink = sink_ref[h]
            m = jnp.maximum(jnp.max(s, axis=-1, keepdims=True), sink)
            p = jnp.exp2(s - m)
            den = jnp.sum(p, axis=-1, keepdims=True) + jnp.exp2(sink - m)
            p_sc[h] = (p * (1.0 / den)).astype(BF16)
        for h in range(N_HEADS):
            kvh = h // GQA_GROUP
            oh = jnp.dot(p_sc[h], kvw[:, kdim + kvh * HEAD_DIM:kdim + (kvh + 1) * HEAD_DIM],
                         preferred_element_type=F32)
            o_sc[rows, h * HEAD_DIM:(h + 1) * HEAD_DIM] = oh.astype(BF16)
    kv_sc[(i + 1) % 2] = kv[tq - BLOCK:]
    mo = jnp.dot(o_sc[...], wout_ref[...], preferred_element_type=F32)
    o_ref[...] = x + _rms(mo, post_ref[...])


def _swa(x, pre_g, post_g, w_in, sinks, w_out, rel_bias):
    B, S, D = x.shape
    tq = min(SWA_TQ, S)
    qdim, kdim = N_HEADS * HEAD_DIM, N_KV_HEADS * HEAD_DIM
    assert S % tq == 0 and tq % BLOCK == 0
    dist = (jnp.arange(BLOCK)[:, None] + BLOCK) - jnp.arange(2 * BLOCK)[None, :]
    band = (dist >= 0) & (dist < WINDOW)
    bias = jnp.where(band[None], _bias_lookup(rel_bias, dist) * LOG2E, -jnp.inf)
    body = functools.partial(_swa_body, tq=tq)
    return pl.pallas_call(
        body,
        out_shape=jax.ShapeDtypeStruct((B, S, D), F32),
        grid=(B, S // tq),
        in_specs=[
            pl.BlockSpec((None, tq, D), lambda b, i: (b, i, 0)),
            _resident((1, D)), _resident((1, D)),
            _resident((D, qdim + 2 * kdim)),
            pl.BlockSpec(memory_space=pltpu.SMEM),
            _resident((N_HEADS, BLOCK, 2 * BLOCK)),
            _resident((qdim, D)),
        ],
        out_specs=pl.BlockSpec((None, tq, D), lambda b, i: (b, i, 0)),
        scratch_shapes=[
            pltpu.VMEM((2, BLOCK, 2 * kdim), BF16),
            pltpu.VMEM((tq, qdim), BF16),
            pltpu.VMEM((N_HEADS, BLOCK, 2 * BLOCK), F32),
            pltpu.VMEM((N_HEADS, BLOCK, 2 * BLOCK), BF16),
        ],
        compiler_params=pltpu.CompilerParams(
            dimension_semantics=("arbitrary", "arbitrary"),
            vmem_limit_bytes=V7X_VMEM_LIMIT_BYTES),
        name="swa",
    )(x, pre_g.reshape(1, D), post_g.reshape(1, D), w_in.astype(BF16), sinks.astype(F32) * LOG2E,
      bias, w_out.astype(BF16))


GMLP_TQ = 256


def _gmlp_body(x_ref, pre_ref, post_ref, win_ref, lng_ref, lnb_ref, ws_ref, bs_ref, wout_ref,
               o_ref, gated_sc, *, tq):
    x = x_ref[...]
    width = wout_ref.shape[0]
    gdim = width // GMLP_GROUPS
    xn = _rms(x, pre_ref[...]).astype(BF16)
    h = jax.nn.gelu(jnp.dot(xn, win_ref[...], preferred_element_type=F32))
    u, v = h[:, :width], h[:, width:]
    mu = jnp.mean(v, axis=-1, keepdims=True)
    var = jnp.mean(jnp.square(v - mu), axis=-1, keepdims=True)
    vn = ((v - mu) * lax.rsqrt(var + EPS) * lng_ref[...] + lnb_ref[...]).astype(BF16)
    for c in range(tq // GMLP_CHUNK):
        rows = slice(c * GMLP_CHUNK, (c + 1) * GMLP_CHUNK)
        for g in range(GMLP_GROUPS):
            cols = slice(g * gdim, (g + 1) * gdim)
            mixed = jnp.dot(ws_ref[g], vn[rows, cols], preferred_element_type=F32) + bs_ref[g]
            gated_sc[rows, cols] = (u[rows, cols] * mixed).astype(BF16)
    mo = jnp.dot(gated_sc[...], wout_ref[...], preferred_element_type=F32)
    o_ref[...] = x + _rms(mo, post_ref[...])


def _gmlp(x, pre_g, post_g, w_in, ln_g, ln_b, w_s, b_s, w_out):
    B, S, D = x.shape
    width = w_out.shape[0]
    gdim = width // GMLP_GROUPS
    tq = min(GMLP_TQ, S)
    assert S % tq == 0 and tq % GMLP_CHUNK == 0 and gdim == V7X_LANES
    causal = jnp.tril(jnp.ones((GMLP_CHUNK, GMLP_CHUNK), dtype=bool))
    ws = jnp.where(causal, w_s, jnp.zeros_like(w_s)).astype(BF16)
    bs = jnp.broadcast_to(b_s[:, :, None], (GMLP_GROUPS, GMLP_CHUNK, gdim)).astype(F32)
    body = functools.partial(_gmlp_body, tq=tq)
    return pl.pallas_call(
        body,
        out_shape=jax.ShapeDtypeStruct((B, S, D), F32),
        grid=(B, S // tq),
        in_specs=[
            pl.BlockSpec((None, tq, D), lambda b, i: (b, i, 0)),
            _resident((1, D)), _resident((1, D)),
            _resident((D, 2 * width)),
            _resident((1, width)), _resident((1, width)),
            _resident((GMLP_GROUPS, GMLP_CHUNK, GMLP_CHUNK)),
            _resident((GMLP_GROUPS, GMLP_CHUNK, gdim)),
            _resident((width, D)),
        ],
        out_specs=pl.BlockSpec((None, tq, D), lambda b, i: (b, i, 0)),
        scratch_shapes=[pltpu.VMEM((tq, width), BF16)],
        compiler_params=pltpu.CompilerParams(
            dimension_semantics=("parallel", "parallel"),
            vmem_limit_bytes=V7X_VMEM_LIMIT_BYTES),
        name="gmlp",
    )(x, pre_g.reshape(1, D), post_g.reshape(1, D), w_in.astype(BF16),
      ln_g.reshape(1, width), ln_b.reshape(1, width), ws, bs, w_out.astype(BF16))


DSA_TQ = 256
DSA_KC = 256
DSA_SMALL = 256
DSA_AUG = 16
INT_MIN = -2 ** 31
NEG_BIG = -1e30
COUNT_LANES = 4


def _dsa_proj_body(x_ref, pre_ref, wq_ref, wqi_ref, wsm_ref, kvg_ref, kig_ref, wuk_ref,
                   qabs_ref, ckv_ref, ckvt_ref, qidx_ref, kidx_ref, widxt_ref, *, tq):
    xn = _rms(x_ref[...], pre_ref[...]).astype(BF16)
    q = jnp.dot(xn, wq_ref[...], preferred_element_type=F32).astype(BF16)
    for h in range(N_HEADS):
        qa = jnp.dot(q[:, h * HEAD_DIM:(h + 1) * HEAD_DIM], wuk_ref[h],
                     preferred_element_type=F32) * (HEAD_DIM ** -0.5 * LOG2E)
        qa = qa.astype(BF16)
        for blk in range(tq // BLOCK):
            qabs_ref[blk, h * BLOCK:(h + 1) * BLOCK, :] = qa[blk * BLOCK:(blk + 1) * BLOCK]
    qi = jnp.dot(xn, wqi_ref[...], preferred_element_type=F32).astype(BF16)
    for h in range(IDX_HEADS):
        for blk in range(tq // BLOCK):
            qidx_ref[blk, h * BLOCK:(h + 1) * BLOCK, :] = qi[blk * BLOCK:(blk + 1) * BLOCK,
                                                              h * IDX_DIM:(h + 1) * IDX_DIM]
    sm = jnp.dot(xn, wsm_ref[...], preferred_element_type=F32)
    ckv = _rms(sm[:, :KV_RANK], kvg_ref[...])
    ckv_ref[...] = ckv.astype(BF16)
    ones_row = jnp.where(lax.broadcasted_iota(jnp.int32, (DSA_AUG, DSA_KC), 0) == 0,
                         jnp.float32(1.0), jnp.float32(0.0)).astype(BF16)
    for cc in range(tq // DSA_KC):
        ckvt_ref[cc, :KV_RANK, :] = jnp.transpose(ckv[cc * DSA_KC:(cc + 1) * DSA_KC]).astype(BF16)
        ckvt_ref[cc, KV_RANK:, :] = ones_row
    kidx_ref[...] = _rms(sm[:, KV_RANK:KV_RANK + IDX_DIM], kig_ref[...]).astype(BF16)
    tail = jnp.transpose(sm[:, KV_RANK:2 * KV_RANK])
    widxt_ref[...] = tail[IDX_DIM:IDX_DIM + IDX_HEADS, :] * (IDX_HEADS ** -0.5 * IDX_DIM ** -0.5)


def _dsa_attn_body(x_ref, qabs_ref, qidx_ref, widxt_ref, kidx_ref, ckv_ref, ckvt_ref, bias_ref,
                   tri_ref, wuv_ref, wout_ref, post_ref, o_ref,
                   keys_sc, acc_sc, m_sc, o_sc, am_sc, p_sc, alpha_sc, *, top_k):
    n = pl.program_id(1)
    kc = DSA_KC
    nch = (n + 2) // 2
    lane = lax.broadcasted_iota(jnp.int32, (kc, BLOCK), 1)
    row = lax.broadcasted_iota(jnp.int32, (kc, BLOCK), 0)
    qpos = n * BLOCK + lane
    nt = (((1,), (1,)), ((), ()))

    def p1(c, carry):
        base = pl.multiple_of(c * kc, kc)
        xt = lax.dot_general(kidx_ref[pl.ds(base, kc), :], qidx_ref[...], nt,
                             preferred_element_type=F32)
        acc = jnp.zeros((kc, BLOCK), F32)
        for h in range(IDX_HEADS):
            acc = acc + widxt_ref[h:h + 1, :] * jnp.maximum(xt[:, h * BLOCK:(h + 1) * BLOCK], 0.0)
        bits = pltpu.bitcast(acc, jnp.int32)
        key = bits ^ ((bits >> 31) & 0x7FFFFFFF)
        keys_sc[pl.ds(base, kc), :] = jnp.where(c * kc + row <= qpos, key, INT_MIN)
        return carry

    lax.fori_loop(0, nch, p1, 0)

    def count_ge(cand):
        def body(c, cnts):
            k = keys_sc[pl.ds(pl.multiple_of(c * kc, kc), kc), :]
            cnts = list(cnts)
            for j in range(kc // 8):
                a = j % COUNT_LANES
                cnts[a] = jnp.where(k[j * 8:(j + 1) * 8, :] >= cand, cnts[a] + 1, cnts[a])
            return tuple(cnts)
        zero = jnp.zeros((8, BLOCK), jnp.int32)
        cnts = lax.fori_loop(0, nch, body, (zero,) * COUNT_LANES)
        total = functools.reduce(lambda u, v: u + v, cnts)
        return jnp.sum(total, axis=0, keepdims=True, dtype=jnp.int32)

    def bisect(it, prefix):
        cand = prefix ^ jnp.left_shift(jnp.int32(1), 31 - it)
        return jnp.where(count_ge(cand) >= top_k, cand, prefix)

    thr = lax.fori_loop(0, 32, bisect, jnp.full((1, BLOCK), INT_MIN, jnp.int32))
    need = (top_k - count_ge(thr + 1)).astype(F32)

    acc_sc[...] = jnp.zeros_like(acc_sc)
    m_sc[...] = jnp.full_like(m_sc, NEG_BIG)

    def probs(c, slot, eq_seen):
        base = pl.multiple_of(c * kc, kc)
        key = keys_sc[pl.ds(base, kc), :]
        eq = key == thr
        eqf = jnp.where(eq, jnp.float32(1.0), jnp.float32(0.0))
        rank = jnp.dot(tri_ref[...], eqf.astype(BF16), preferred_element_type=F32) + eq_seen
        sel = jnp.logical_or(key > thr, jnp.logical_and(eq, rank < need))
        sel = jnp.logical_and(sel, key != INT_MIN)
        am_sc[...] = jnp.where(sel, jnp.float32(0.0), jnp.float32(-jnp.inf))
        ckc = ckv_ref[pl.ds(base, kc), :]
        off = pl.multiple_of(2 * kc - jnp.minimum(n * BLOCK - c * kc, 2 * kc), BLOCK)
        for h in range(N_HEADS):
            lt = lax.dot_general(ckc, qabs_ref[h * BLOCK:(h + 1) * BLOCK, :], nt,
                                 preferred_element_type=F32)
            lt = lt + (am_sc[...] + bias_ref[h, pl.ds(off, kc), :])
            m_old = m_sc[h]
            m_new = jnp.maximum(m_old, jnp.max(lt, axis=0, keepdims=True))
            m_sc[h] = m_new
            alpha_sc[slot, h] = jnp.exp2(m_old - m_new)
            p_sc[slot, h] = jnp.exp2(lt - m_new).astype(BF16)
        return eq_seen + jnp.sum(eqf, axis=0, keepdims=True)

    def accumulate(c, slot):
        ckt = ckvt_ref[c]
        for h in range(N_HEADS):
            acc_sc[h] = alpha_sc[slot, h] * acc_sc[h] + jnp.dot(
                ckt, p_sc[slot, h], preferred_element_type=F32)

    def p3(c, eq_seen):
        eq_seen = probs(c, c % 2, eq_seen)
        accumulate(c - 1, (c - 1) % 2)
        return eq_seen

    lax.fori_loop(1, nch, p3, probs(0, 0, jnp.zeros((1, BLOCK), F32)))
    accumulate(nch - 1, (nch - 1) % 2)

    for h in range(N_HEADS):
        a = acc_sc[h]
        ot = (a[:KV_RANK] * (1.0 / a[KV_RANK:KV_RANK + 1])).astype(BF16)
        o_sc[h * HEAD_DIM:(h + 1) * HEAD_DIM, :] = jnp.dot(
            wuv_ref[h], ot, preferred_element_type=F32).astype(BF16)
    mt = jnp.dot(wout_ref[...], o_sc[...], preferred_element_type=F32)
    o_ref[...] = x_ref[...] + _rms(jnp.transpose(mt), post_ref[...])


def _dsa(x, pre_g, post_g, w_in, kv_g, ki_g, w_uk, w_uv, w_out, rel_bias):
    B, S, D = x.shape
    qdim = N_HEADS * HEAD_DIM
    qidim = IDX_HEADS * IDX_DIM
    tq = min(DSA_TQ, S)
    top_k = min(TOPK_MAX, S // 4)
    assert S % tq == 0 and tq % DSA_KC == 0 and KV_RANK == BLOCK and DSA_KC == 2 * BLOCK
    c0, c1, c2, c3 = qdim, qdim + KV_RANK, qdim + KV_RANK + qidim, qdim + KV_RANK + qidim + IDX_DIM
    wq = w_in[:, :c0].astype(BF16)
    wqi = w_in[:, c1:c2].astype(BF16)
    wsm = jnp.concatenate([w_in[:, c0:c1], w_in[:, c2:c3], w_in[:, c3:]], axis=1)
    wsm = jnp.pad(wsm, ((0, 0), (0, DSA_SMALL - wsm.shape[1]))).astype(BF16)
    wuk = jnp.transpose(w_uk, (1, 2, 0)).astype(BF16)
    wuv = jnp.transpose(w_uv, (1, 2, 0)).astype(BF16)
    wout_t = jnp.transpose(w_out).astype(BF16)
    nblk = S // BLOCK
    raug = KV_RANK + DSA_AUG

    proj = pl.pallas_call(
        functools.partial(_dsa_proj_body, tq=tq),
        out_shape=[
            jax.ShapeDtypeStruct((B, nblk, N_HEADS * BLOCK, KV_RANK), BF16),
            jax.ShapeDtypeStruct((B, S, KV_RANK), BF16),
            jax.ShapeDtypeStruct((B, S // DSA_KC, raug, DSA_KC), BF16),
            jax.ShapeDtypeStruct((B, nblk, IDX_HEADS * BLOCK, IDX_DIM), BF16),
            jax.ShapeDtypeStruct((B, S, IDX_DIM), BF16),
            jax.ShapeDtypeStruct((B, IDX_HEADS, S), F32),
        ],
        grid=(B, S // tq),
        in_specs=[
            pl.BlockSpec((None, tq, D), lambda b, i: (b, i, 0)),
            _resident((1, D)),
            _resident((D, qdim)), _resident((D, qidim)), _resident((D, DSA_SMALL)),
            _resident((1, KV_RANK)), _resident((1, IDX_DIM)),
            _resident((N_HEADS, HEAD_DIM, KV_RANK)),
        ],
        out_specs=[
            pl.BlockSpec((None, tq // BLOCK, N_HEADS * BLOCK, KV_RANK), lambda b, i: (b, i, 0, 0)),
            pl.BlockSpec((None, tq, KV_RANK), lambda b, i: (b, i, 0)),
            pl.BlockSpec((None, tq // DSA_KC, raug, DSA_KC), lambda b, i: (b, i, 0, 0)),
            pl.BlockSpec((None, tq // BLOCK, IDX_HEADS * BLOCK, IDX_DIM), lambda b, i: (b, i, 0, 0)),
            pl.BlockSpec((None, tq, IDX_DIM), lambda b, i: (b, i, 0)),
            pl.BlockSpec((None, IDX_HEADS, tq), lambda b, i: (b, 0, i)),
        ],
        compiler_params=pltpu.CompilerParams(
            dimension_semantics=("parallel", "parallel"),
            vmem_limit_bytes=V7X_VMEM_LIMIT_BYTES),
        name="dsa_proj",
    )
    qabs, ckv, ckvt, qidx, kidx, widxt = proj(
        x, pre_g.reshape(1, D), wq, wqi, wsm, kv_g.reshape(1, KV_RANK), ki_g.reshape(1, IDX_DIM), wuk)

    dist = 2 * DSA_KC + jnp.arange(BLOCK)[None, :] - jnp.arange(3 * DSA_KC)[:, None]
    far = rel_bias[REL_BUCKETS - 1].astype(F32)
    bias_t = (_bias_lookup(rel_bias, dist) - far[:, None, None]) * LOG2E
    tri = jnp.tril(jnp.ones((DSA_KC, DSA_KC), F32), -1).astype(BF16)

    attn = pl.pallas_call(
        functools.partial(_dsa_attn_body, top_k=top_k),
        out_shape=jax.ShapeDtypeStruct((B, S, D), F32),
        grid=(B, nblk),
        in_specs=[
            pl.BlockSpec((None, BLOCK, D), lambda b, n: (b, n, 0)),
            pl.BlockSpec((None, None, N_HEADS * BLOCK, KV_RANK), lambda b, n: (b, n, 0, 0)),
            pl.BlockSpec((None, None, IDX_HEADS * BLOCK, IDX_DIM), lambda b, n: (b, n, 0, 0)),
            pl.BlockSpec((None, IDX_HEADS, BLOCK), lambda b, n: (b, 0, n)),
            pl.BlockSpec((None, S, IDX_DIM), lambda b, n: (b, 0, 0)),
            pl.BlockSpec((None, S, KV_RANK), lambda b, n: (b, 0, 0)),
            pl.BlockSpec((None, S // DSA_KC, raug, DSA_KC), lambda b, n: (b, 0, 0, 0)),
            _resident((N_HEADS, 3 * DSA_KC, BLOCK)),
            _resident((DSA_KC, DSA_KC)),
            _resident((N_HEADS, HEAD_DIM, KV_RANK)),
            _resident((D, qdim)),
            _resident((1, D)),
        ],
        out_specs=pl.BlockSpec((None, BLOCK, D), lambda b, n: (b, n, 0)),
        scratch_shapes=[
            pltpu.VMEM((S, BLOCK), jnp.int32),
            pltpu.VMEM((N_HEADS, raug, BLOCK), F32),
            pltpu.VMEM((N_HEADS, 1, BLOCK), F32),
            pltpu.VMEM((qdim, BLOCK), BF16),
            pltpu.VMEM((DSA_KC, BLOCK), F32),
            pltpu.VMEM((2, N_HEADS, DSA_KC, BLOCK), BF16),
            pltpu.VMEM((2, N_HEADS, 1, BLOCK), F32),
        ],
        compiler_params=pltpu.CompilerParams(
            dimension_semantics=("parallel", "parallel"),
            vmem_limit_bytes=V7X_VMEM_LIMIT_BYTES),
        name="dsa_attn",
    )
    return attn(x, qabs, qidx, widxt, kidx, ckv, ckvt, bias_t, tri, wuv, wout_t, post_g.reshape(1, D))


def kernel(x, rel_bias, l0_norm_pre_mix, l0_norm_post_mix, l0_attn_w_in, l0_attn_sinks, l0_attn_w_out, l0_norm_pre_ffn, l0_norm_post_ffn, l0_ffn_w_up, l0_ffn_conv_w, l0_ffn_conv_b, l0_ffn_w_down, l1_norm_pre_mix, l1_norm_post_mix, l1_gmlp_w_in, l1_gmlp_ln_g, l1_gmlp_ln_b, l1_gmlp_w_s, l1_gmlp_b_s, l1_gmlp_w_out, l1_norm_pre_ffn, l1_norm_post_ffn, l1_ffn_w_up, l1_ffn_conv_w, l1_ffn_conv_b, l1_ffn_w_down, l2_norm_pre_mix, l2_norm_post_mix, l2_dsa_w_in, l2_dsa_kv_norm, l2_dsa_kidx_norm, l2_dsa_w_uk, l2_dsa_w_uv, l2_dsa_w_out, l2_norm_pre_ffn, l2_norm_post_ffn, l2_ffn_w_up, l2_ffn_conv_w, l2_ffn_conv_b, l2_ffn_w_down, l3_norm_pre_mix, l3_norm_post_mix, l3_attn_w_in, l3_attn_sinks, l3_attn_w_out, l3_norm_pre_ffn, l3_norm_post_ffn, l3_ffn_w_up, l3_ffn_conv_w, l3_ffn_conv_b, l3_ffn_w_down):
    x = _swa(x, l0_norm_pre_mix, l0_norm_post_mix, l0_attn_w_in, l0_attn_sinks, l0_attn_w_out, rel_bias)
    x = _ffn(x, l0_norm_pre_ffn, l0_norm_post_ffn, l0_ffn_w_up, l0_ffn_conv_w, l0_ffn_conv_b, l0_ffn_w_down)
    x = _gmlp(x, l1_norm_pre_mix, l1_norm_post_mix, l1_gmlp_w_in, l1_gmlp_ln_g, l1_gmlp_ln_b, l1_gmlp_w_s, l1_gmlp_b_s, l1_gmlp_w_out)
    x = _ffn(x, l1_norm_pre_ffn, l1_norm_post_ffn, l1_ffn_w_up, l1_ffn_conv_w, l1_ffn_conv_b, l1_ffn_w_down)
    x = _dsa(x, l2_norm_pre_mix, l2_norm_post_mix, l2_dsa_w_in, l2_dsa_kv_norm, l2_dsa_kidx_norm, l2_dsa_w_uk, l2_dsa_w_uv, l2_dsa_w_out, rel_bias)
    x = _ffn(x, l2_norm_pre_ffn, l2_norm_post_ffn, l2_ffn_w_up, l2_ffn_conv_w, l2_ffn_conv_b, l2_ffn_w_down)
    x = _swa(x, l3_norm_pre_mix, l3_norm_post_mix, l3_attn_w_in, l3_attn_sinks, l3_attn_w_out, rel_bias)
    x = _ffn(x, l3_norm_pre_ffn, l3_norm_post_ffn, l3_ffn_w_up, l3_ffn_conv_w, l3_ffn_conv_b, l3_ffn_w_down)
    return x
```

```python
import functools
import math

import jax
import jax.numpy as jnp
from jax import lax
from jax.experimental import pallas as pl
from jax.experimental.pallas import tpu as pltpu

F32 = jnp.float32
BF16 = jnp.bfloat16

EPS = 1e-6
N_HEADS = 16
HEAD_DIM = 64
N_KV_HEADS = 4
GQA_GROUP = N_HEADS // N_KV_HEADS
WINDOW = 128
BLOCK = 128
GMLP_CHUNK = 128
GMLP_GROUPS = 8
KV_RANK = 128
IDX_HEADS = 8
IDX_DIM = 64
TOPK_MAX = 256
REL_BUCKETS = 32
REL_MAX_DIST = 128
CONV_WIDTH = 3

V7X_LANES = 128
V7X_BF16_SUBLANES = 16
V7X_VMEM_LIMIT_BYTES = 56 * 1024 * 1024

LOG2E = math.log2(math.e)


def _rms(xf, g):
    ms = jnp.mean(xf * xf, axis=-1, keepdims=True)
    return xf * lax.rsqrt(ms + EPS) * g


def _resident(shape):
    nd = len(shape)
    return pl.BlockSpec(shape, lambda *_: (0,) * nd, pipeline_mode=pl.Buffered(1))


FFN_TM = 512
FFN_TN = 256
FFN_HALO = V7X_BF16_SUBLANES


def _ffn_body(x_ref, xh_ref, pre_ref, post_ref, wg_ref, wv_ref, cw_ref, cb_ref, wd_ref,
              o_ref, xn_sc, h_sc, a_sc, acc_sc, *, tm, nch, halo):
    i = pl.program_id(1)
    pre = pre_ref[...]
    xn_sc[halo:, :] = _rms(x_ref[...], pre).astype(BF16)
    xh = jnp.where(i > 0, _rms(xh_ref[...], pre), 0.0)
    xn_sc[:halo, :] = xh.astype(BF16)
    acc_sc[...] = jnp.zeros_like(acc_sc)

    def up(j, slot):
        xn = xn_sc[...]
        h_sc[slot, 0] = jnp.dot(xn, wg_ref[j], preferred_element_type=F32)
        h_sc[slot, 1] = jnp.dot(xn, wv_ref[j], preferred_element_type=F32)

    def conv(slot, part, w, b):
        h0 = h_sc[slot, part, halo:, :]
        h1 = h_sc[slot, part, halo - 1:halo - 1 + tm, :]
        h2 = h_sc[slot, part, halo - 2:halo - 2 + tm, :]
        return h0 * w[2:3, :] + h1 * w[1:2, :] + h2 * w[0:1, :] + b

    def act(j, slot):
        g = conv(slot, 0, cw_ref[j], cb_ref[j])
        v = conv(slot, 1, cw_ref[nch + j], cb_ref[nch + j])
        a_sc[slot] = (g * (1.0 / (1.0 + jnp.exp(-g))) * v).astype(BF16)

    def down(j, slot):
        acc_sc[...] += jnp.dot(a_sc[slot], wd_ref[j], preferred_element_type=F32)

    for j in range(nch + 2):
        if j >= 2:
            down(j - 2, j % 2)
        if 1 <= j <= nch:
            act(j - 1, (j - 1) % 2)
        if j < nch:
            up(j, j % 2)
    o_ref[...] = x_ref[...] + _rms(acc_sc[...], post_ref[...])


def _ffn(x, pre_g, post_g, w_up, conv_w, conv_b, w_down):
    B, S, D = x.shape
    dff = w_down.shape[0]
    tm, tn, halo = min(FFN_TM, S), FFN_TN, FFN_HALO
    nch = dff // tn
    assert dff % tn == 0 and S % tm == 0 and tm % halo == 0
    wup = w_up.astype(BF16).reshape(D, 2, nch, tn)
    wg = jnp.transpose(wup[:, 0], (1, 0, 2))
    wv = jnp.transpose(wup[:, 1], (1, 0, 2))
    wd = w_down.astype(BF16).reshape(nch, tn, D)
    cw = jnp.transpose(conv_w.reshape(CONV_WIDTH, 2 * nch, tn), (1, 0, 2))
    cb = conv_b.reshape(2 * nch, 1, tn)
    body = functools.partial(_ffn_body, tm=tm, nch=nch, halo=halo)
    return pl.pallas_call(
        body,
        out_shape=jax.ShapeDtypeStruct((B, S, D), F32),
        grid=(B, S // tm),
        in_specs=[
            pl.BlockSpec((None, tm, D), lambda b, i: (b, i, 0)),
            pl.BlockSpec((None, halo, D), lambda b, i: (b, jnp.maximum(i * (tm // halo) - 1, 0), 0)),
            _resident((1, D)), _resident((1, D)),
            _resident((nch, D, tn)), _resident((nch, D, tn)),
            _resident((2 * nch, CONV_WIDTH, tn)), _resident((2 * nch, 1, tn)),
            _resident((nch, tn, D)),
        ],
        out_specs=pl.BlockSpec((None, tm, D), lambda b, i: (b, i, 0)),
        scratch_shapes=[
            pltpu.VMEM((halo + tm, D), BF16),
            pltpu.VMEM((2, 2, halo + tm, tn), F32),
            pltpu.VMEM((2, tm, tn), BF16),
            pltpu.VMEM((tm, D), F32),
        ],
        compiler_params=pltpu.CompilerParams(
            dimension_semantics=("parallel", "parallel"),
            vmem_limit_bytes=V7X_VMEM_LIMIT_BYTES,
            ),
        name="ffn",
    )(x, x, pre_g.reshape(1, D), post_g.reshape(1, D), wg, wv, cw, cb, wd)


def _rel_bucket(dist):
    max_exact = REL_BUCKETS // 2
    d = jnp.maximum(dist, 0)
    df = jnp.maximum(d, 1).astype(F32)
    large = max_exact + (jnp.log(df / max_exact) / math.log(REL_MAX_DIST / max_exact)
                         * (REL_BUCKETS - max_exact)).astype(jnp.int32)
    large = jnp.minimum(large, REL_BUCKETS - 1)
    return jnp.where(d < max_exact, d, large)


def _bias_lookup(rel_bias, dist):
    onehot = _rel_bucket(dist)[None, ..., None] == jnp.arange(REL_BUCKETS)
    table = jnp.transpose(rel_bias.astype(F32)).reshape((rel_bias.shape[1],) + (1,) * dist.ndim + (REL_BUCKETS,))
    return jnp.sum(jnp.where(onehot, table, 0.0), axis=-1)


SWA_TQ = 256


def _swa_body(x_ref, pre_ref, post_ref, win_ref, sink_ref, bias_ref, wout_ref, o_ref,
              kv_sc, o_sc, s_sc, p_sc, *, tq):
    i = pl.program_id(1)
    qdim = N_HEADS * HEAD_DIM
    kdim = N_KV_HEADS * HEAD_DIM

    @pl.when(i == 0)
    def _():
        kv_sc[0] = jnp.zeros(kv_sc.shape[1:], BF16)

    x = x_ref[...]
    xn = _rms(x, pre_ref[...]).astype(BF16)
    qkv = jnp.dot(xn, win_ref[...], preferred_element_type=F32)
    q = (qkv[:, :qdim] * (HEAD_DIM ** -0.5 * LOG2E)).astype(BF16)
    kv = qkv[:, qdim:].astype(BF16)
    col = lax.broadcasted_iota(jnp.int32, (BLOCK, 2 * BLOCK), 1)
    first_masked = jnp.where(i == 0, BLOCK, 0)
    for r in range(tq // BLOCK):
        rows = slice(r * BLOCK, (r + 1) * BLOCK)
        kv_prev = kv_sc[i % 2] if r == 0 else kv[(r - 1) * BLOCK:r * BLOCK]
        kvw = jnp.concatenate([kv_prev, kv[rows]], axis=0)
        for h in range(N_HEADS):
            kvh = h // GQA_GROUP
            s = lax.dot_general(q[rows, h * HEAD_DIM:(h + 1) * HEAD_DIM],
                                kvw[:, kvh * HEAD_DIM:(kvh + 1) * HEAD_DIM],
                                (((1,), (1,)), ((), ())), preferred_element_type=F32) + bias_ref[h]
            if r == 0:
                s = jnp.where(col < first_masked, -jnp.inf, s)
            s_sc[h] = s
        for h in range(N_HEADS):
            s = s_sc[h]
            sink = sink_ref[h]
            m = jnp.maximum(jnp.max(s, axis=-1, keepdims=True), sink)
            p = jnp.exp2(s - m)
            den = jnp.sum(p, axis=-1, keepdims=True) + jnp.exp2(sink - m)
            p_sc[h] = (p * (1.0 / den)).astype(BF16)
        for h in range(N_HEADS):
            kvh = h // GQA_GROUP
            oh = jnp.dot(p_sc[h], kvw[:, kdim + kvh * HEAD_DIM:kdim + (kvh + 1) * HEAD_DIM],
                         preferred_element_type=F32)
            o_sc[rows, h * HEAD_DIM:(h + 1) * HEAD_DIM] = oh.astype(BF16)
    kv_sc[(i + 1) % 2] = kv[tq - BLOCK:]
    mo = jnp.dot(o_sc[...], wout_ref[...], preferred_element_type=F32)
    o_ref[...] = x + _rms(mo, post_ref[...])


def _swa(x, pre_g, post_g, w_in, sinks, w_out, rel_bias):
    B, S, D = x.shape
    tq = min(SWA_TQ, S)
    qdim, kdim = N_HEADS * HEAD_DIM, N_KV_HEADS * HEAD_DIM
    assert S % tq == 0 and tq % BLOCK == 0
    dist = (jnp.arange(BLOCK)[:, None] + BLOCK) - jnp.arange(2 * BLOCK)[None, :]
    band = (dist >= 0) & (dist < WINDOW)
    bias = jnp.where(band[None], _bias_lookup(rel_bias, dist) * LOG2E, -jnp.inf)
    body = functools.partial(_swa_body, tq=tq)
    return pl.pallas_call(
        body,
        out_shape=jax.ShapeDtypeStruct((B, S, D), F32),
        grid=(B, S // tq),
        in_specs=[
            pl.BlockSpec((None, tq, D), lambda b, i: (b, i, 0)),
            _resident((1, D)), _resident((1, D)),
            _resident((D, qdim + 2 * kdim)),
            pl.BlockSpec(memory_space=pltpu.SMEM),
            _resident((N_HEADS, BLOCK, 2 * BLOCK)),
            _resident((qdim, D)),
        ],
        out_specs=pl.BlockSpec((None, tq, D), lambda b, i: (b, i, 0)),
        scratch_shapes=[
            pltpu.VMEM((2, BLOCK, 2 * kdim), BF16),
            pltpu.VMEM((tq, qdim), BF16),
            pltpu.VMEM((N_HEADS, BLOCK, 2 * BLOCK), F32),
            pltpu.VMEM((N_HEADS, BLOCK, 2 * BLOCK), BF16),
        ],
        compiler_params=pltpu.CompilerParams(
            dimension_semantics=("arbitrary", "arbitrary"),
            vmem_limit_bytes=V7X_VMEM_LIMIT_BYTES),
        name="swa",
    )(x, pre_g.reshape(1, D), post_g.reshape(1, D), w_in.astype(BF16), sinks.astype(F32) * LOG2E,
      bias, w_out.astype(BF16))


GMLP_TQ = 256


def _gmlp_body(x_ref, pre_ref, post_ref, win_ref, lng_ref, lnb_ref, ws_ref, bs_ref, wout_ref,
               o_ref, gated_sc, *, tq):
    x = x_ref[...]
    width = wout_ref.shape[0]
    gdim = width // GMLP_GROUPS
    xn = _rms(x, pre_ref[...]).astype(BF16)
    h = jax.nn.gelu(jnp.dot(xn, win_ref[...], preferred_element_type=F32))
    u, v = h[:, :width], h[:, width:]
    mu = jnp.mean(v, axis=-1, keepdims=True)
    var = jnp.mean(jnp.square(v - mu), axis=-1, keepdims=True)
    vn = ((v - mu) * lax.rsqrt(var + EPS) * lng_ref[...] + lnb_ref[...]).astype(BF16)
    for c in range(tq // GMLP_CHUNK):
        rows = slice(c * GMLP_CHUNK, (c + 1) * GMLP_CHUNK)
        for g in range(GMLP_GROUPS):
            cols = slice(g * gdim, (g + 1) * gdim)
            mixed = jnp.dot(ws_ref[g], vn[rows, cols], preferred_element_type=F32) + bs_ref[g]
            gated_sc[rows, cols] = (u[rows, cols] * mixed).astype(BF16)
    mo = jnp.dot(gated_sc[...], wout_ref[...], preferred_element_type=F32)
    o_ref[...] = x + _rms(mo, post_ref[...])


def _gmlp(x, pre_g, post_g, w_in, ln_g, ln_b, w_s, b_s, w_out):
    B, S, D = x.shape
    width = w_out.shape[0]
    gdim = width // GMLP_GROUPS
    tq = min(GMLP_TQ, S)
    assert S % tq == 0 and tq % GMLP_CHUNK == 0 and gdim == V7X_LANES
    causal = jnp.tril(jnp.ones((GMLP_CHUNK, GMLP_CHUNK), dtype=bool))
    ws = jnp.where(causal, w_s, jnp.zeros_like(w_s)).astype(BF16)
    bs = jnp.broadcast_to(b_s[:, :, None], (GMLP_GROUPS, GMLP_CHUNK, gdim)).astype(F32)
    body = functools.partial(_gmlp_body, tq=tq)
    return pl.pallas_call(
        body,
        out_shape=jax.ShapeDtypeStruct((B, S, D), F32),
        grid=(B, S // tq),
        in_specs=[
            pl.BlockSpec((None, tq, D), lambda b, i: (b, i, 0)),
            _resident((1, D)), _resident((1, D)),
            _resident((D, 2 * width)),
            _resident((1, width)), _resident((1, width)),
            _resident((GMLP_GROUPS, GMLP_CHUNK, GMLP_CHUNK)),
            _resident((GMLP_GROUPS, GMLP_CHUNK, gdim)),
            _resident((width, D)),
        ],
        out_specs=pl.BlockSpec((None, tq, D), lambda b, i: (b, i, 0)),
        scratch_shapes=[pltpu.VMEM((tq, width), BF16)],
        compiler_params=pltpu.CompilerParams(
            dimension_semantics=("parallel", "parallel"),
            vmem_limit_bytes=V7X_VMEM_LIMIT_BYTES),
        name="gmlp",
    )(x, pre_g.reshape(1, D), post_g.reshape(1, D), w_in.astype(BF16),
      ln_g.reshape(1, width), ln_b.reshape(1, width), ws, bs, w_out.astype(BF16))


DSA_TQ = 256
DSA_KC = 256
DSA_SMALL = 256
DSA_AUG = 16
INT_MIN = -2 ** 31
NEG_BIG = -1e30
COUNT_LANES = 4


def _dsa_proj_body(x_ref, pre_ref, wq_ref, wqi_ref, wsm_ref, kvg_ref, kig_ref, wuk_ref,
                   qabs_ref, ckv_ref, ckvt_ref, qidx_ref, kidx_ref, widxt_ref, *, tq):
    xn = _rms(x_ref[...], pre_ref[...]).astype(BF16)
    q = jnp.dot(xn, wq_ref[...], preferred_element_type=F32).astype(BF16)
    for h in range(N_HEADS):
        qa = jnp.dot(q[:, h * HEAD_DIM:(h + 1) * HEAD_DIM], wuk_ref[h],
                     preferred_element_type=F32) * (HEAD_DIM ** -0.5 * LOG2E)
        qa = qa.astype(BF16)
        for blk in range(tq // BLOCK):
            qabs_ref[blk, h * BLOCK:(h + 1) * BLOCK, :] = qa[blk * BLOCK:(blk + 1) * BLOCK]
    qi = jnp.dot(xn, wqi_ref[...], preferred_element_type=F32).astype(BF16)
    for h in range(IDX_HEADS):
        for blk in range(tq // BLOCK):
            qidx_ref[blk, h * BLOCK:(h + 1) * BLOCK, :] = qi[blk * BLOCK:(blk + 1) * BLOCK,
                                                              h * IDX_DIM:(h + 1) * IDX_DIM]
    sm = jnp.dot(xn, wsm_ref[...], preferred_element_type=F32)
    ckv = _rms(sm[:, :KV_RANK], kvg_ref[...])
    ckv_ref[...] = ckv.astype(BF16)
    ones_row = jnp.where(lax.broadcasted_iota(jnp.int32, (DSA_AUG, DSA_KC), 0) == 0,
                         jnp.float32(1.0), jnp.float32(0.0)).astype(BF16)
    for cc in range(tq // DSA_KC):
        ckvt_ref[cc, :KV_RANK, :] = jnp.transpose(ckv[cc * DSA_KC:(cc + 1) * DSA_KC]).astype(BF16)
        ckvt_ref[cc, KV_RANK:, :] = ones_row
    kidx_ref[...] = _rms(sm[:, KV_RANK:KV_RANK + IDX_DIM], kig_ref[...]).astype(BF16)
    tail = jnp.transpose(sm[:, KV_RANK:2 * KV_RANK])
    widxt_ref[...] = tail[IDX_DIM:IDX_DIM + IDX_HEADS, :] * (IDX_HEADS ** -0.5 * IDX_DIM ** -0.5)


def _dsa_attn_body(x_ref, qabs_ref, qidx_ref, widxt_ref, kidx_ref, ckv_ref, ckvt_ref, bias_ref,
                   tri_ref, wuv_ref, wout_ref, post_ref, o_ref,
                   keys_sc, hi_sc, lo_sc, acc_sc, m_sc, o_sc, am_sc, p_sc, alpha_sc, *, top_k):
    n = pl.program_id(1)
    kc = DSA_KC
    nch = (n + 2) // 2
    lane = lax.broadcasted_iota(jnp.int32, (kc, BLOCK), 1)
    row = lax.broadcasted_iota(jnp.int32, (kc, BLOCK), 0)
    qpos = n * BLOCK + lane
    nt = (((1,), (1,)), ((), ()))

    def p1(c, carry):
        base = pl.multiple_of(c * kc, kc)
        xt = lax.dot_general(kidx_ref[pl.ds(base, kc), :], qidx_ref[...], nt,
                             preferred_element_type=F32)
        acc = jnp.zeros((kc, BLOCK), F32)
        for h in range(IDX_HEADS):
            acc = acc + widxt_ref[h:h + 1, :] * jnp.maximum(xt[:, h * BLOCK:(h + 1) * BLOCK], 0.0)
        bits = pltpu.bitcast(acc, jnp.int32)
        key = bits ^ ((bits >> 31) & 0x7FFFFFFF)
        key = jnp.where(c * kc + row <= qpos, key, INT_MIN)
        keys_sc[pl.ds(base, kc), :] = key
        hi_sc[pl.ds(base, kc), :] = (key >> 16).astype(jnp.int16)
        return carry

    lax.fori_loop(0, nch, p1, 0)

    def chunks(body, init):
        return lax.fori_loop(0, nch, lambda c, v: body(pl.multiple_of(c * kc, kc), v), init)

    def count_ge16(src_ref, cand):
        c16 = cand.astype(jnp.int16)

        def body(base, cnts):
            v = src_ref[pl.ds(base, kc), :]
            cnts = list(cnts)
            for j in range(kc // 16):
                a = j % COUNT_LANES
                cnts[a] = jnp.where(v[j * 16:(j + 1) * 16, :] >= c16, cnts[a] + jnp.int16(1), cnts[a])
            return tuple(cnts)

        cnts = chunks(body, (jnp.zeros((16, BLOCK), jnp.int16),) * COUNT_LANES)
        total = functools.reduce(lambda u, v: u + v, [c.astype(jnp.int32) for c in cnts])
        return jnp.sum(total, axis=0, keepdims=True, dtype=jnp.int32)

    def kth_largest16(src_ref, kth):
        def bit(it, u):
            cu = u | jnp.left_shift(jnp.int32(1), 15 - it)
            return jnp.where(count_ge16(src_ref, cu - 2 ** 15) >= kth, cu, u)
        return lax.fori_loop(0, 16, bit, jnp.zeros((1, BLOCK), jnp.int32)) - 2 ** 15

    def count_gt32(cand):
        def body(base, cnts):
            k = keys_sc[pl.ds(base, kc), :]
            cnts = list(cnts)
            for j in range(kc // 8):
                a = j % COUNT_LANES
                cnts[a] = jnp.where(k[j * 8:(j + 1) * 8, :] > cand, cnts[a] + 1, cnts[a])
            return tuple(cnts)
        cnts = chunks(body, (jnp.zeros((8, BLOCK), jnp.int32),) * COUNT_LANES)
        return jnp.sum(functools.reduce(lambda u, v: u + v, cnts), axis=0, keepdims=True, dtype=jnp.int32)

    hi = kth_largest16(hi_sc, top_k)
    above = jnp.where(hi >= 2 ** 15 - 1, 0, count_ge16(hi_sc, jnp.minimum(hi + 1, 2 ** 15 - 1)))

    def low_halves(base, carry):
        k = keys_sc[pl.ds(base, kc), :]
        lo = jnp.where((k >> 16) == hi, (k & 0xFFFF) - 2 ** 15, -2 ** 15)
        lo_sc[pl.ds(base, kc), :] = lo.astype(jnp.int16)
        return carry

    chunks(low_halves, 0)
    lo = kth_largest16(lo_sc, top_k - above)
    thr = hi * 2 ** 16 + (lo + 2 ** 15)
    need = (top_k - count_gt32(thr)).astype(F32)

    acc_sc[...] = jnp.zeros_like(acc_sc)
    m_sc[...] = jnp.full_like(m_sc, NEG_BIG)

    def probs(c, slot, eq_seen):
        base = pl.multiple_of(c * kc, kc)
        key = keys_sc[pl.ds(base, kc), :]
        eq = key == thr
        eqf = jnp.where(eq, jnp.float32(1.0), jnp.float32(0.0))
        rank = jnp.dot(tri_ref[...], eqf.astype(BF16), preferred_element_type=F32) + eq_seen
        sel = jnp.logical_or(key > thr, jnp.logical_and(eq, rank < need))
        sel = jnp.logical_and(sel, key != INT_MIN)
        am_sc[...] = jnp.where(sel, jnp.float32(0.0), jnp.float32(-jnp.inf))
        ckc = ckv_ref[pl.ds(base, kc), :]
        off = pl.multiple_of(2 * kc - jnp.minimum(n * BLOCK - c * kc, 2 * kc), BLOCK)
        for h in range(N_HEADS):
            lt = lax.dot_general(ckc, qabs_ref[h * BLOCK:(h + 1) * BLOCK, :], nt,
                                 preferred_element_type=F32)
            lt = lt + (am_sc[...] + bias_ref[h, pl.ds(off, kc), :])
            m_old = m_sc[h]
            m_new = jnp.maximum(m_old, jnp.max(lt, axis=0, keepdims=True))
            m_sc[h] = m_new
            alpha_sc[slot, h] = jnp.exp2(m_old - m_new)
            p_sc[slot, h] = jnp.exp2(lt - m_new).astype(BF16)
        return eq_seen + jnp.sum(eqf, axis=0, keepdims=True)

    def accumulate(c, slot):
        ckt = ckvt_ref[c]
        for h in range(N_HEADS):
            acc_sc[h] = alpha_sc[slot, h] * acc_sc[h] + jnp.dot(
                ckt, p_sc[slot, h], preferred_element_type=F32)

    def step01(c, eq_seen):
        accumulate(c - 1, 0)
        return probs(c, 1, eq_seen)

    def step10(c, eq_seen):
        accumulate(c - 1, 1)
        return probs(c, 0, eq_seen)

    def pair(k, eq_seen):
        return step10(2 * k + 2, step01(2 * k + 1, eq_seen))

    eq_seen = lax.fori_loop(0, (nch - 1) // 2, pair, probs(0, 0, jnp.zeros((1, BLOCK), F32)))

    @pl.when(nch % 2 == 0)
    def _():
        step01(nch - 1, eq_seen)
        accumulate(nch - 1, 1)

    @pl.when(nch % 2 == 1)
    def _():
        accumulate(nch - 1, 0)

    for h in range(N_HEADS):
        a = acc_sc[h]
        ot = (a[:KV_RANK] * (1.0 / a[KV_RANK:KV_RANK + 1])).astype(BF16)
        o_sc[h * HEAD_DIM:(h + 1) * HEAD_DIM, :] = jnp.dot(
            wuv_ref[h], ot, preferred_element_type=F32).astype(BF16)
    mt = jnp.dot(wout_ref[...], o_sc[...], preferred_element_type=F32)
    o_ref[...] = x_ref[...] + _rms(jnp.transpose(mt), post_ref[...])


def _dsa(x, pre_g, post_g, w_in, kv_g, ki_g, w_uk, w_uv, w_out, rel_bias):
    B, S, D = x.shape
    qdim = N_HEADS * HEAD_DIM
    qidim = IDX_HEADS * IDX_DIM
    tq = min(DSA_TQ, S)
    top_k = min(TOPK_MAX, S // 4)
    assert S % tq == 0 and tq % DSA_KC == 0 and KV_RANK == BLOCK and DSA_KC == 2 * BLOCK
    c0, c1, c2, c3 = qdim, qdim + KV_RANK, qdim + KV_RANK + qidim, qdim + KV_RANK + qidim + IDX_DIM
    wq = w_in[:, :c0].astype(BF16)
    wqi = w_in[:, c1:c2].astype(BF16)
    wsm = jnp.concatenate([w_in[:, c0:c1], w_in[:, c2:c3], w_in[:, c3:]], axis=1)
    wsm = jnp.pad(wsm, ((0, 0), (0, DSA_SMALL - wsm.shape[1]))).astype(BF16)
    wuk = jnp.transpose(w_uk, (1, 2, 0)).astype(BF16)
    wuv = jnp.transpose(w_uv, (1, 2, 0)).astype(BF16)
    wout_t = jnp.transpose(w_out).astype(BF16)
    nblk = S // BLOCK
    raug = KV_RANK + DSA_AUG

    proj = pl.pallas_call(
        functools.partial(_dsa_proj_body, tq=tq),
        out_shape=[
            jax.ShapeDtypeStruct((B, nblk, N_HEADS * BLOCK, KV_RANK), BF16),
            jax.ShapeDtypeStruct((B, S, KV_RANK), BF16),
            jax.ShapeDtypeStruct((B, S // DSA_KC, raug, DSA_KC), BF16),
            jax.ShapeDtypeStruct((B, nblk, IDX_HEADS * BLOCK, IDX_DIM), BF16),
            jax.ShapeDtypeStruct((B, S, IDX_DIM), BF16),
            jax.ShapeDtypeStruct((B, IDX_HEADS, S), F32),
        ],
        grid=(B, S // tq),
        in_specs=[
            pl.BlockSpec((None, tq, D), lambda b, i: (b, i, 0)),
            _resident((1, D)),
            _resident((D, qdim)), _resident((D, qidim)), _resident((D, DSA_SMALL)),
            _resident((1, KV_RANK)), _resident((1, IDX_DIM)),
            _resident((N_HEADS, HEAD_DIM, KV_RANK)),
        ],
        out_specs=[
            pl.BlockSpec((None, tq // BLOCK, N_HEADS * BLOCK, KV_RANK), lambda b, i: (b, i, 0, 0)),
            pl.BlockSpec((None, tq, KV_RANK), lambda b, i: (b, i, 0)),
            pl.BlockSpec((None, tq // DSA_KC, raug, DSA_KC), lambda b, i: (b, i, 0, 0)),
            pl.BlockSpec((None, tq // BLOCK, IDX_HEADS * BLOCK, IDX_DIM), lambda b, i: (b, i, 0, 0)),
            pl.BlockSpec((None, tq, IDX_DIM), lambda b, i: (b, i, 0)),
            pl.BlockSpec((None, IDX_HEADS, tq), lambda b, i: (b, 0, i)),
        ],
        compiler_params=pltpu.CompilerParams(
            dimension_semantics=("parallel", "parallel"),
            vmem_limit_bytes=V7X_VMEM_LIMIT_BYTES),
        name="dsa_proj",
    )
    qabs, ckv, ckvt, qidx, kidx, widxt = proj(
        x, pre_g.reshape(1, D), wq, wqi, wsm, kv_g.reshape(1, KV_RANK), ki_g.reshape(1, IDX_DIM), wuk)

    dist = 2 * DSA_KC + jnp.arange(BLOCK)[None, :] - jnp.arange(3 * DSA_KC)[:, None]
    far = rel_bias[REL_BUCKETS - 1].astype(F32)
    bias_t = (_bias_lookup(rel_bias, dist) - far[:, None, None]) * LOG2E
    tri = jnp.tril(jnp.ones((DSA_KC, DSA_KC), F32), -1).astype(BF16)

    attn = pl.pallas_call(
        functools.partial(_dsa_attn_body, top_k=top_k),
        out_shape=jax.ShapeDtypeStruct((B, S, D), F32),
        grid=(B, nblk),
        in_specs=[
            pl.BlockSpec((None, BLOCK, D), lambda b, n: (b, n, 0)),
            pl.BlockSpec((None, None, N_HEADS * BLOCK, KV_RANK), lambda b, n: (b, n, 0, 0)),
            pl.BlockSpec((None, None, IDX_HEADS * BLOCK, IDX_DIM), lambda b, n: (b, n, 0, 0)),
            pl.BlockSpec((None, IDX_HEADS, BLOCK), lambda b, n: (b, 0, n)),
            pl.BlockSpec((None, S, IDX_DIM), lambda b, n: (b, 0, 0)),
            pl.BlockSpec((None, S, KV_RANK), lambda b, n: (b, 0, 0)),
            pl.BlockSpec((None, S // DSA_KC, raug, DSA_KC), lambda b, n: (b, 0, 0, 0)),
            _resident((N_HEADS, 3 * DSA_KC, BLOCK)),
            _resident((DSA_KC, DSA_KC)),
            _resident((N_HEADS, HEAD_DIM, KV_RANK)),
            _resident((D, qdim)),
            _resident((1, D)),
        ],
        out_specs=pl.BlockSpec((None, BLOCK, D), lambda b, n: (b, n, 0)),
        scratch_shapes=[
            pltpu.VMEM((S, BLOCK), jnp.int32),
            pltpu.VMEM((S, BLOCK), jnp.int16),
            pltpu.VMEM((S, BLOCK), jnp.int16),
            pltpu.VMEM((N_HEADS, raug, BLOCK), F32),
            pltpu.VMEM((N_HEADS, 1, BLOCK), F32),
            pltpu.VMEM((qdim, BLOCK), BF16),
            pltpu.VMEM((DSA_KC, BLOCK), F32),
            pltpu.VMEM((2, N_HEADS, DSA_KC, BLOCK), BF16),
            pltpu.VMEM((2, N_HEADS, 1, BLOCK), F32),
        ],
        compiler_params=pltpu.CompilerParams(
            dimension_semantics=("parallel", "parallel"),
            vmem_limit_bytes=V7X_VMEM_LIMIT_BYTES),
        name="dsa_attn",
    )
    return attn(x, qabs, qidx, widxt, kidx, ckv, ckvt, bias_t, tri, wuv, wout_t, post_g.reshape(1, D))


def kernel(x, rel_bias, l0_norm_pre_mix, l0_norm_post_mix, l0_attn_w_in, l0_attn_sinks, l0_attn_w_out, l0_norm_pre_ffn, l0_norm_post_ffn, l0_ffn_w_up, l0_ffn_conv_w, l0_ffn_conv_b, l0_ffn_w_down, l1_norm_pre_mix, l1_norm_post_mix, l1_gmlp_w_in, l1_gmlp_ln_g, l1_gmlp_ln_b, l1_gmlp_w_s, l1_gmlp_b_s, l1_gmlp_w_out, l1_norm_pre_ffn, l1_norm_post_ffn, l1_ffn_w_up, l1_ffn_conv_w, l1_ffn_conv_b, l1_ffn_w_down, l2_norm_pre_mix, l2_norm_post_mix, l2_dsa_w_in, l2_dsa_kv_norm, l2_dsa_kidx_norm, l2_dsa_w_uk, l2_dsa_w_uv, l2_dsa_w_out, l2_norm_pre_ffn, l2_norm_post_ffn, l2_ffn_w_up, l2_ffn_conv_w, l2_ffn_conv_b, l2_ffn_w_down, l3_norm_pre_mix, l3_norm_post_mix, l3_attn_w_in, l3_attn_sinks, l3_attn_w_out, l3_norm_pre_ffn, l3_norm_post_ffn, l3_ffn_w_up, l3_ffn_conv_w, l3_ffn_conv_b, l3_ffn_w_down):
    x = _swa(x, l0_norm_pre_mix, l0_norm_post_mix, l0_attn_w_in, l0_attn_sinks, l0_attn_w_out, rel_bias)
    x = _ffn(x, l0_norm_pre_ffn, l0_norm_post_ffn, l0_ffn_w_up, l0_ffn_conv_w, l0_ffn_conv_b, l0_ffn_w_down)
    x = _gmlp(x, l1_norm_pre_mix, l1_norm_post_mix, l1_gmlp_w_in, l1_gmlp_ln_g, l1_gmlp_ln_b, l1_gmlp_w_s, l1_gmlp_b_s, l1_gmlp_w_out)
    x = _ffn(x, l1_norm_pre_ffn, l1_norm_post_ffn, l1_ffn_w_up, l1_ffn_conv_w, l1_ffn_conv_b, l1_ffn_w_down)
    x = _dsa(x, l2_norm_pre_mix, l2_norm_post_mix, l2_dsa_w_in, l2_dsa_kv_norm, l2_dsa_kidx_norm, l2_dsa_w_uk, l2_dsa_w_uv, l2_dsa_w_out, rel_bias)
    x = _ffn(x, l2_norm_pre_ffn, l2_norm_post_ffn, l2_ffn_w_up, l2_ffn_conv_w, l2_ffn_conv_b, l2_ffn_w_down)
    x = _swa(x, l3_norm_pre_mix, l3_norm_post_mix, l3_attn_w_in, l3_attn_sinks, l3_attn_w_out, rel_bias)
    x = _ffn(x, l3_norm_pre_ffn, l3_norm_post_ffn, l3_ffn_w_up, l3_ffn_conv_w, l3_ffn_conv_b, l3_ffn_w_down)
    return x
```

```python
import functools
import math

import jax
import jax.numpy as jnp
from jax import lax
from jax.experimental import pallas as pl
from jax.experimental.pallas import tpu as pltpu

F32 = jnp.float32
BF16 = jnp.bfloat16

EPS = 1e-6
N_HEADS = 16
HEAD_DIM = 64
N_KV_HEADS = 4
GQA_GROUP = N_HEADS // N_KV_HEADS
WINDOW = 128
BLOCK = 128
GMLP_CHUNK = 128
GMLP_GROUPS = 8
KV_RANK = 128
IDX_HEADS = 8
IDX_DIM = 64
TOPK_MAX = 256
REL_BUCKETS = 32
REL_MAX_DIST = 128
CONV_WIDTH = 3

V7X_LANES = 128
V7X_BF16_SUBLANES = 16
V7X_VMEM_LIMIT_BYTES = 56 * 1024 * 1024

LOG2E = math.log2(math.e)


def _rms(xf, g):
    ms = jnp.mean(xf * xf, axis=-1, keepdims=True)
    return xf * lax.rsqrt(ms + EPS) * g


def _resident(shape):
    nd = len(shape)
    return pl.BlockSpec(shape, lambda *_: (0,) * nd, pipeline_mode=pl.Buffered(1))


FFN_TM = 512
FFN_TN = 256
FFN_HALO = V7X_BF16_SUBLANES


def _ffn_body(x_ref, xh_ref, pre_ref, post_ref, wg_ref, wv_ref, cw_ref, cb_ref, wd_ref,
              o_ref, xn_sc, h_sc, a_sc, acc_sc, *, tm, nch, halo):
    i = pl.program_id(1)
    pre = pre_ref[...]
    xn_sc[halo:, :] = _rms(x_ref[...], pre).astype(BF16)
    xh = jnp.where(i > 0, _rms(xh_ref[...], pre), 0.0)
    xn_sc[:halo, :] = xh.astype(BF16)
    acc_sc[...] = jnp.zeros_like(acc_sc)

    def up(j, slot):
        xn = xn_sc[...]
        h_sc[slot, 0] = jnp.dot(xn, wg_ref[j], preferred_element_type=F32)
        h_sc[slot, 1] = jnp.dot(xn, wv_ref[j], preferred_element_type=F32)

    def conv(slot, part, w, b):
        h0 = h_sc[slot, part, halo:, :]
        h1 = h_sc[slot, part, halo - 1:halo - 1 + tm, :]
        h2 = h_sc[slot, part, halo - 2:halo - 2 + tm, :]
        return h0 * w[2:3, :] + h1 * w[1:2, :] + h2 * w[0:1, :] + b

    def act(j, slot):
        g = conv(slot, 0, cw_ref[j], cb_ref[j])
        v = conv(slot, 1, cw_ref[nch + j], cb_ref[nch + j])
        a_sc[slot] = (g * (1.0 / (1.0 + jnp.exp(-g))) * v).astype(BF16)

    def down(j, slot):
        acc_sc[...] += jnp.dot(a_sc[slot], wd_ref[j], preferred_element_type=F32)

    for j in range(nch + 2):
        if j >= 2:
            down(j - 2, j % 2)
        if 1 <= j <= nch:
            act(j - 1, (j - 1) % 2)
        if j < nch:
            up(j, j % 2)
    o_ref[...] = x_ref[...] + _rms(acc_sc[...], post_ref[...])


def _ffn(x, pre_g, post_g, w_up, conv_w, conv_b, w_down):
    B, S, D = x.shape
    dff = w_down.shape[0]
    tm, tn, halo = min(FFN_TM, S), FFN_TN, FFN_HALO
    nch = dff // tn
    assert dff % tn == 0 and S % tm == 0 and tm % halo == 0
    wup = w_up.astype(BF16).reshape(D, 2, nch, tn)
    wg = jnp.transpose(wup[:, 0], (1, 0, 2))
    wv = jnp.transpose(wup[:, 1], (1, 0, 2))
    wd = w_down.astype(BF16).reshape(nch, tn, D)
    cw = jnp.transpose(conv_w.reshape(CONV_WIDTH, 2 * nch, tn), (1, 0, 2))
    cb = conv_b.reshape(2 * nch, 1, tn)
    body = functools.partial(_ffn_body, tm=tm, nch=nch, halo=halo)
    return pl.pallas_call(
        body,
        out_shape=jax.ShapeDtypeStruct((B, S, D), F32),
        grid=(B, S // tm),
        in_specs=[
            pl.BlockSpec((None, tm, D), lambda b, i: (b, i, 0)),
            pl.BlockSpec((None, halo, D), lambda b, i: (b, jnp.maximum(i * (tm // halo) - 1, 0), 0)),
            _resident((1, D)), _resident((1, D)),
            _resident((nch, D, tn)), _resident((nch, D, tn)),
            _resident((2 * nch, CONV_WIDTH, tn)), _resident((2 * nch, 1, tn)),
            _resident((nch, tn, D)),
        ],
        out_specs=pl.BlockSpec((None, tm, D), lambda b, i: (b, i, 0)),
        scratch_shapes=[
            pltpu.VMEM((halo + tm, D), BF16),
            pltpu.VMEM((2, 2, halo + tm, tn), F32),
            pltpu.VMEM((2, tm, tn), BF16),
            pltpu.VMEM((tm, D), F32),
        ],
        compiler_params=pltpu.CompilerParams(
            dimension_semantics=("parallel", "parallel"),
            vmem_limit_bytes=V7X_VMEM_LIMIT_BYTES,
            ),
        name="ffn",
    )(x, x, pre_g.reshape(1, D), post_g.reshape(1, D), wg, wv, cw, cb, wd)


def _rel_bucket(dist):
    max_exact = REL_BUCKETS // 2
    d = jnp.maximum(dist, 0)
    df = jnp.maximum(d, 1).astype(F32)
    large = max_exact + (jnp.log(df / max_exact) / math.log(REL_MAX_DIST / max_exact)
                         * (REL_BUCKETS - max_exact)).astype(jnp.int32)
    large = jnp.minimum(large, REL_BUCKETS - 1)
    return jnp.where(d < max_exact, d, large)


def _bias_lookup(rel_bias, dist):
    onehot = _rel_bucket(dist)[None, ..., None] == jnp.arange(REL_BUCKETS)
    table = jnp.transpose(rel_bias.astype(F32)).reshape((rel_bias.shape[1],) + (1,) * dist.ndim + (REL_BUCKETS,))
    return jnp.sum(jnp.where(onehot, table, 0.0), axis=-1)


def _bias_toeplitz(rel_bias, base, rows, cols):
    vals = _bias_lookup(rel_bias, jnp.arange(base - rows + 1, base + cols))
    row = lambda t: lax.dynamic_slice_in_dim(vals, rows - 1 - t, cols, axis=1)
    return jnp.transpose(jax.vmap(row)(jnp.arange(rows)), (1, 0, 2))


SWA_TQ = 256


def _swa_body(x_ref, pre_ref, post_ref, win_ref, sink_ref, bias_ref, wout_ref, o_ref,
              kv_sc, o_sc, s_sc, p_sc, *, tq):
    i = pl.program_id(1)
    qdim = N_HEADS * HEAD_DIM
    kdim = N_KV_HEADS * HEAD_DIM

    @pl.when(i == 0)
    def _():
        kv_sc[0] = jnp.zeros(kv_sc.shape[1:], BF16)

    x = x_ref[...]
    xn = _rms(x, pre_ref[...]).astype(BF16)
    qkv = jnp.dot(xn, win_ref[...], preferred_element_type=F32)
    q = (qkv[:, :qdim] * (HEAD_DIM ** -0.5 * LOG2E)).astype(BF16)
    kv = qkv[:, qdim:].astype(BF16)
    col = lax.broadcasted_iota(jnp.int32, (BLOCK, 2 * BLOCK), 1)
    first_masked = jnp.where(i == 0, BLOCK, 0)
    for r in range(tq // BLOCK):
        rows = slice(r * BLOCK, (r + 1) * BLOCK)
        kv_prev = kv_sc[i % 2] if r == 0 else kv[(r - 1) * BLOCK:r * BLOCK]
        kvw = jnp.concatenate([kv_prev, kv[rows]], axis=0)
        for h in range(N_HEADS):
            kvh = h // GQA_GROUP
            s = lax.dot_general(q[rows, h * HEAD_DIM:(h + 1) * HEAD_DIM],
                                kvw[:, kvh * HEAD_DIM:(kvh + 1) * HEAD_DIM],
                                (((1,), (1,)), ((), ())), preferred_element_type=F32) + bias_ref[h]
            if r == 0:
                s = jnp.where(col < first_masked, -jnp.inf, s)
            s_sc[h] = s
        for h in range(N_HEADS):
            s = s_sc[h]
            sink = sink_ref[h]
            m = jnp.maximum(jnp.max(s, axis=-1, keepdims=True), sink)
            p = jnp.exp2(s - m)
            den = jnp.sum(p, axis=-1, keepdims=True) + jnp.exp2(sink - m)
            p_sc[h] = (p * (1.0 / den)).astype(BF16)
        for h in range(N_HEADS):
            kvh = h // GQA_GROUP
            oh = jnp.dot(p_sc[h], kvw[:, kdim + kvh * HEAD_DIM:kdim + (kvh + 1) * HEAD_DIM],
                         preferred_element_type=F32)
            o_sc[rows, h * HEAD_DIM:(h + 1) * HEAD_DIM] = oh.astype(BF16)
    kv_sc[(i + 1) % 2] = kv[tq - BLOCK:]
    mo = jnp.dot(o_sc[...], wout_ref[...], preferred_element_type=F32)
    o_ref[...] = x + _rms(mo, post_ref[...])


def _swa(x, pre_g, post_g, w_in, sinks, w_out, rel_bias):
    B, S, D = x.shape
    tq = min(SWA_TQ, S)
    qdim, kdim = N_HEADS * HEAD_DIM, N_KV_HEADS * HEAD_DIM
    assert S % tq == 0 and tq % BLOCK == 0
    dist = (jnp.arange(BLOCK)[:, None] + BLOCK) - jnp.arange(2 * BLOCK)[None, :]
    band = (dist >= 0) & (dist < WINDOW)
    bias = jnp.transpose(_bias_toeplitz(rel_bias, BLOCK, 2 * BLOCK, BLOCK), (0, 2, 1))
    bias = jnp.where(band[None], bias * LOG2E, -jnp.inf)
    body = functools.partial(_swa_body, tq=tq)
    return pl.pallas_call(
        body,
        out_shape=jax.ShapeDtypeStruct((B, S, D), F32),
        grid=(B, S // tq),
        in_specs=[
            pl.BlockSpec((None, tq, D), lambda b, i: (b, i, 0)),
            _resident((1, D)), _resident((1, D)),
            _resident((D, qdim + 2 * kdim)),
            pl.BlockSpec(memory_space=pltpu.SMEM),
            _resident((N_HEADS, BLOCK, 2 * BLOCK)),
            _resident((qdim, D)),
        ],
        out_specs=pl.BlockSpec((None, tq, D), lambda b, i: (b, i, 0)),
        scratch_shapes=[
            pltpu.VMEM((2, BLOCK, 2 * kdim), BF16),
            pltpu.VMEM((tq, qdim), BF16),
            pltpu.VMEM((N_HEADS, BLOCK, 2 * BLOCK), F32),
            pltpu.VMEM((N_HEADS, BLOCK, 2 * BLOCK), BF16),
        ],
        compiler_params=pltpu.CompilerParams(
            dimension_semantics=("arbitrary", "arbitrary"),
            vmem_limit_bytes=V7X_VMEM_LIMIT_BYTES),
        name="swa",
    )(x, pre_g.reshape(1, D), post_g.reshape(1, D), w_in.astype(BF16), sinks.astype(F32) * LOG2E,
      bias, w_out.astype(BF16))


GMLP_TQ = 256


def _gmlp_body(x_ref, pre_ref, post_ref, win_ref, lng_ref, lnb_ref, ws_ref, bs_ref, wout_ref,
               o_ref, gated_sc, *, tq):
    x = x_ref[...]
    width = wout_ref.shape[0]
    gdim = width // GMLP_GROUPS
    xn = _rms(x, pre_ref[...]).astype(BF16)
    h = jax.nn.gelu(jnp.dot(xn, win_ref[...], preferred_element_type=F32))
    u, v = h[:, :width], h[:, width:]
    mu = jnp.mean(v, axis=-1, keepdims=True)
    var = jnp.mean(jnp.square(v - mu), axis=-1, keepdims=True)
    vn = ((v - mu) * lax.rsqrt(var + EPS) * lng_ref[...] + lnb_ref[...]).astype(BF16)
    for c in range(tq // GMLP_CHUNK):
        rows = slice(c * GMLP_CHUNK, (c + 1) * GMLP_CHUNK)
        for g in range(GMLP_GROUPS):
            cols = slice(g * gdim, (g + 1) * gdim)
            mixed = jnp.dot(ws_ref[g], vn[rows, cols], preferred_element_type=F32) + bs_ref[g]
            gated_sc[rows, cols] = (u[rows, cols] * mixed).astype(BF16)
    mo = jnp.dot(gated_sc[...], wout_ref[...], preferred_element_type=F32)
    o_ref[...] = x + _rms(mo, post_ref[...])


def _gmlp(x, pre_g, post_g, w_in, ln_g, ln_b, w_s, b_s, w_out):
    B, S, D = x.shape
    width = w_out.shape[0]
    gdim = width // GMLP_GROUPS
    tq = min(GMLP_TQ, S)
    assert S % tq == 0 and tq % GMLP_CHUNK == 0 and gdim == V7X_LANES
    causal = jnp.tril(jnp.ones((GMLP_CHUNK, GMLP_CHUNK), dtype=bool))
    ws = jnp.where(causal, w_s, jnp.zeros_like(w_s)).astype(BF16)
    bs = jnp.broadcast_to(b_s[:, :, None], (GMLP_GROUPS, GMLP_CHUNK, gdim)).astype(F32)
    body = functools.partial(_gmlp_body, tq=tq)
    return pl.pallas_call(
        body,
        out_shape=jax.ShapeDtypeStruct((B, S, D), F32),
        grid=(B, S // tq),
        in_specs=[
            pl.BlockSpec((None, tq, D), lambda b, i: (b, i, 0)),
            _resident((1, D)), _resident((1, D)),
            _resident((D, 2 * width)),
            _resident((1, width)), _resident((1, width)),
            _resident((GMLP_GROUPS, GMLP_CHUNK, GMLP_CHUNK)),
            _resident((GMLP_GROUPS, GMLP_CHUNK, gdim)),
            _resident((width, D)),
        ],
        out_specs=pl.BlockSpec((None, tq, D), lambda b, i: (b, i, 0)),
        scratch_shapes=[pltpu.VMEM((tq, width), BF16)],
        compiler_params=pltpu.CompilerParams(
            dimension_semantics=("parallel", "parallel"),
            vmem_limit_bytes=V7X_VMEM_LIMIT_BYTES),
        name="gmlp",
    )(x, pre_g.reshape(1, D), post_g.reshape(1, D), w_in.astype(BF16),
      ln_g.reshape(1, width), ln_b.reshape(1, width), ws, bs, w_out.astype(BF16))


DSA_TQ = 256
DSA_KC = 256
DSA_SMALL = 256
DSA_AUG = 16
INT_MIN = -2 ** 31
NEG_BIG = -1e30
COUNT_LANES = 4


def _dsa_proj_body(x_ref, pre_ref, wq_ref, wqi_ref, wsm_ref, kvg_ref, kig_ref, wuk_ref,
                   qabs_ref, ckv_ref, ckvt_ref, qidx_ref, kidx_ref, widxt_ref, *, tq):
    xn = _rms(x_ref[...], pre_ref[...]).astype(BF16)
    q = jnp.dot(xn, wq_ref[...], preferred_element_type=F32).astype(BF16)
    for h in range(N_HEADS):
        qa = jnp.dot(q[:, h * HEAD_DIM:(h + 1) * HEAD_DIM], wuk_ref[h],
                     preferred_element_type=F32) * (HEAD_DIM ** -0.5 * LOG2E)
        qa = qa.astype(BF16)
        for blk in range(tq // BLOCK):
            qabs_ref[blk, h * BLOCK:(h + 1) * BLOCK, :] = qa[blk * BLOCK:(blk + 1) * BLOCK]
    qi = jnp.dot(xn, wqi_ref[...], preferred_element_type=F32).astype(BF16)
    for h in range(IDX_HEADS):
        for blk in range(tq // BLOCK):
            qidx_ref[blk, h * BLOCK:(h + 1) * BLOCK, :] = qi[blk * BLOCK:(blk + 1) * BLOCK,
                                                              h * IDX_DIM:(h + 1) * IDX_DIM]
    sm = jnp.dot(xn, wsm_ref[...], preferred_element_type=F32)
    ckv = _rms(sm[:, :KV_RANK], kvg_ref[...])
    ckv_ref[...] = ckv.astype(BF16)
    ones_row = jnp.where(lax.broadcasted_iota(jnp.int32, (DSA_AUG, DSA_KC), 0) == 0,
                         jnp.float32(1.0), jnp.float32(0.0)).astype(BF16)
    for cc in range(tq // DSA_KC):
        ckvt_ref[cc, :KV_RANK, :] = jnp.transpose(ckv[cc * DSA_KC:(cc + 1) * DSA_KC]).astype(BF16)
        ckvt_ref[cc, KV_RANK:, :] = ones_row
    kidx_ref[...] = _rms(sm[:, KV_RANK:KV_RANK + IDX_DIM], kig_ref[...]).astype(BF16)
    tail = jnp.transpose(sm[:, KV_RANK:2 * KV_RANK])
    widxt_ref[...] = tail[IDX_DIM:IDX_DIM + IDX_HEADS, :] * (IDX_HEADS ** -0.5 * IDX_DIM ** -0.5)


def _dsa_attn_body(x_ref, qabs_ref, qidx_ref, widxt_ref, kidx_ref, ckv_ref, ckvt_ref, bias_ref,
                   tri_ref, wuv_ref, wout_ref, post_ref, o_ref,
                   keys_sc, hi_sc, lo_sc, acc_sc, m_sc, o_sc, am_sc, p_sc, alpha_sc, *, top_k):
    n = pl.program_id(1)
    kc = DSA_KC
    nch = (n + 2) // 2
    lane = lax.broadcasted_iota(jnp.int32, (kc, BLOCK), 1)
    row = lax.broadcasted_iota(jnp.int32, (kc, BLOCK), 0)
    qpos = n * BLOCK + lane
    nt = (((1,), (1,)), ((), ()))

    def p1(c, carry):
        base = pl.multiple_of(c * kc, kc)
        xt = lax.dot_general(kidx_ref[pl.ds(base, kc), :], qidx_ref[...], nt,
                             preferred_element_type=F32)
        acc = jnp.zeros((kc, BLOCK), F32)
        for h in range(IDX_HEADS):
            acc = acc + widxt_ref[h:h + 1, :] * jnp.maximum(xt[:, h * BLOCK:(h + 1) * BLOCK], 0.0)
        bits = pltpu.bitcast(acc, jnp.int32)
        key = bits ^ ((bits >> 31) & 0x7FFFFFFF)
        key = jnp.where(c * kc + row <= qpos, key, INT_MIN)
        keys_sc[pl.ds(base, kc), :] = key
        hi_sc[pl.ds(base, kc), :] = (key >> 16).astype(jnp.int16)
        return carry

    lax.fori_loop(0, nch, p1, 0)

    def chunks(body, init):
        return lax.fori_loop(0, nch, lambda c, v: body(pl.multiple_of(c * kc, kc), v), init)

    def count_ge16(src_ref, cand):
        c16 = cand.astype(jnp.int16)

        def body(base, cnts):
            v = src_ref[pl.ds(base, kc), :]
            cnts = list(cnts)
            for j in range(kc // 16):
                a = j % COUNT_LANES
                cnts[a] = jnp.where(v[j * 16:(j + 1) * 16, :] >= c16, cnts[a] + jnp.int16(1), cnts[a])
            return tuple(cnts)

        cnts = chunks(body, (jnp.zeros((16, BLOCK), jnp.int16),) * COUNT_LANES)
        total = functools.reduce(lambda u, v: u + v, [c.astype(jnp.int32) for c in cnts])
        return jnp.sum(total, axis=0, keepdims=True, dtype=jnp.int32)

    def kth_largest16(src_ref, kth):
        def bit(it, u):
            cu = u | jnp.left_shift(jnp.int32(1), 15 - it)
            return jnp.where(count_ge16(src_ref, cu - 2 ** 15) >= kth, cu, u)
        return lax.fori_loop(0, 16, bit, jnp.zeros((1, BLOCK), jnp.int32)) - 2 ** 15

    def count_gt32(cand):
        def body(base, cnts):
            k = keys_sc[pl.ds(base, kc), :]
            cnts = list(cnts)
            for j in range(kc // 8):
                a = j % COUNT_LANES
                cnts[a] = jnp.where(k[j * 8:(j + 1) * 8, :] > cand, cnts[a] + 1, cnts[a])
            return tuple(cnts)
        cnts = chunks(body, (jnp.zeros((8, BLOCK), jnp.int32),) * COUNT_LANES)
        return jnp.sum(functools.reduce(lambda u, v: u + v, cnts), axis=0, keepdims=True, dtype=jnp.int32)

    hi = kth_largest16(hi_sc, top_k)
    above = jnp.where(hi >= 2 ** 15 - 1, 0, count_ge16(hi_sc, jnp.minimum(hi + 1, 2 ** 15 - 1)))

    def low_halves(base, carry):
        k = keys_sc[pl.ds(base, kc), :]
        lo = jnp.where((k >> 16) == hi, (k & 0xFFFF) - 2 ** 15, -2 ** 15)
        lo_sc[pl.ds(base, kc), :] = lo.astype(jnp.int16)
        return carry

    chunks(low_halves, 0)
    lo = kth_largest16(lo_sc, top_k - above)
    thr = hi * 2 ** 16 + (lo + 2 ** 15)
    need = (top_k - count_gt32(thr)).astype(F32)

    acc_sc[...] = jnp.zeros_like(acc_sc)
    m_sc[...] = jnp.full_like(m_sc, NEG_BIG)

    def probs(c, slot, eq_seen, far=False):
        base = pl.multiple_of(c * kc, kc)
        key = keys_sc[pl.ds(base, kc), :]
        eq = key == thr
        eqf = jnp.where(eq, jnp.float32(1.0), jnp.float32(0.0))
        rank = jnp.dot(tri_ref[...], eqf.astype(BF16), preferred_element_type=F32) + eq_seen
        sel = jnp.logical_or(key > thr, jnp.logical_and(eq, rank < need))
        sel = jnp.logical_and(sel, key != INT_MIN)
        am_sc[...] = jnp.where(sel, jnp.float32(0.0), jnp.float32(-jnp.inf))
        ckc = ckv_ref[pl.ds(base, kc), :]
        off = pl.multiple_of(2 * kc - jnp.minimum(n * BLOCK - c * kc, 2 * kc), BLOCK)
        for hp in range(N_HEADS // 2):
            lt2 = lax.dot_general(ckc, qabs_ref[hp * 2 * BLOCK:(hp + 1) * 2 * BLOCK, :], nt,
                                  preferred_element_type=F32)
            for j in range(2):
                h = 2 * hp + j
                ls = slice(j * BLOCK, (j + 1) * BLOCK)
                if far:
                    lt = lt2[:, ls] + am_sc[...]
                else:
                    lt = lt2[:, ls] + (am_sc[...] + bias_ref[h, pl.ds(off, kc), :])
                m_old = m_sc[h]
                m_new = jnp.maximum(m_old, jnp.max(lt, axis=0, keepdims=True))
                m_sc[h] = m_new
                alpha_sc[slot, hp, :, ls] = jnp.exp2(m_old - m_new)
                p_sc[slot, hp, :, ls] = jnp.exp2(lt - m_new).astype(BF16)
        return eq_seen + jnp.sum(eqf, axis=0, keepdims=True)

    def accumulate(c, slot):
        ckt = ckvt_ref[c]
        for hp in range(N_HEADS // 2):
            acc_sc[hp] = alpha_sc[slot, hp] * acc_sc[hp] + jnp.dot(
                ckt, p_sc[slot, hp], preferred_element_type=F32)

    def step01(c, eq_seen, far=False):
        accumulate(c - 1, 0)
        return probs(c, 1, eq_seen, far)

    def step10(c, eq_seen, far=False):
        accumulate(c - 1, 1)
        return probs(c, 0, eq_seen, far)

    def pair(far, k, eq_seen):
        return step10(2 * k + 2, step01(2 * k + 1, eq_seen, far), far)

    nfar = jnp.maximum((n - 2) // 2, 0)
    kfar = jnp.maximum((nfar - 1) // 2, 0)
    eq_seen = probs(0, 0, jnp.zeros((1, BLOCK), F32))
    eq_seen = lax.fori_loop(0, kfar, functools.partial(pair, True), eq_seen)
    eq_seen = lax.fori_loop(kfar, (nch - 1) // 2, functools.partial(pair, False), eq_seen)

    @pl.when(nch % 2 == 0)
    def _():
        step01(nch - 1, eq_seen)
        accumulate(nch - 1, 1)

    @pl.when(nch % 2 == 1)
    def _():
        accumulate(nch - 1, 0)

    for hp in range(N_HEADS // 2):
        a = acc_sc[hp]
        ot = (a[:KV_RANK] * (1.0 / a[KV_RANK:KV_RANK + 1])).astype(BF16)
        for j in range(2):
            h = 2 * hp + j
            o_sc[h * HEAD_DIM:(h + 1) * HEAD_DIM, :] = jnp.dot(
                wuv_ref[h], ot[:, j * BLOCK:(j + 1) * BLOCK],
                preferred_element_type=F32).astype(BF16)
    mt = jnp.dot(wout_ref[...], o_sc[...], preferred_element_type=F32)
    o_ref[...] = x_ref[...] + _rms(jnp.transpose(mt), post_ref[...])


def _dsa(x, pre_g, post_g, w_in, kv_g, ki_g, w_uk, w_uv, w_out, rel_bias):
    B, S, D = x.shape
    qdim = N_HEADS * HEAD_DIM
    qidim = IDX_HEADS * IDX_DIM
    tq = min(DSA_TQ, S)
    top_k = min(TOPK_MAX, S // 4)
    assert S % tq == 0 and tq % DSA_KC == 0 and KV_RANK == BLOCK and DSA_KC == 2 * BLOCK
    c0, c1, c2, c3 = qdim, qdim + KV_RANK, qdim + KV_RANK + qidim, qdim + KV_RANK + qidim + IDX_DIM
    wq = w_in[:, :c0].astype(BF16)
    wqi = w_in[:, c1:c2].astype(BF16)
    wsm = jnp.concatenate([w_in[:, c0:c1], w_in[:, c2:c3], w_in[:, c3:]], axis=1)
    wsm = jnp.pad(wsm, ((0, 0), (0, DSA_SMALL - wsm.shape[1]))).astype(BF16)
    wuk = jnp.transpose(w_uk, (1, 2, 0)).astype(BF16)
    wuv = jnp.transpose(w_uv, (1, 2, 0)).astype(BF16)
    wout_t = jnp.transpose(w_out).astype(BF16)
    nblk = S // BLOCK
    raug = KV_RANK + DSA_AUG

    proj = pl.pallas_call(
        functools.partial(_dsa_proj_body, tq=tq),
        out_shape=[
            jax.ShapeDtypeStruct((B, nblk, N_HEADS * BLOCK, KV_RANK), BF16),
            jax.ShapeDtypeStruct((B, S, KV_RANK), BF16),
            jax.ShapeDtypeStruct((B, S // DSA_KC, raug, DSA_KC), BF16),
            jax.ShapeDtypeStruct((B, nblk, IDX_HEADS * BLOCK, IDX_DIM), BF16),
            jax.ShapeDtypeStruct((B, S, IDX_DIM), BF16),
            jax.ShapeDtypeStruct((B, IDX_HEADS, S), F32),
        ],
        grid=(B, S // tq),
        in_specs=[
            pl.BlockSpec((None, tq, D), lambda b, i: (b, i, 0)),
            _resident((1, D)),
            _resident((D, qdim)), _resident((D, qidim)), _resident((D, DSA_SMALL)),
            _resident((1, KV_RANK)), _resident((1, IDX_DIM)),
            _resident((N_HEADS, HEAD_DIM, KV_RANK)),
        ],
        out_specs=[
            pl.BlockSpec((None, tq // BLOCK, N_HEADS * BLOCK, KV_RANK), lambda b, i: (b, i, 0, 0)),
            pl.BlockSpec((None, tq, KV_RANK), lambda b, i: (b, i, 0)),
            pl.BlockSpec((None, tq // DSA_KC, raug, DSA_KC), lambda b, i: (b, i, 0, 0)),
            pl.BlockSpec((None, tq // BLOCK, IDX_HEADS * BLOCK, IDX_DIM), lambda b, i: (b, i, 0, 0)),
            pl.BlockSpec((None, tq, IDX_DIM), lambda b, i: (b, i, 0)),
            pl.BlockSpec((None, IDX_HEADS, tq), lambda b, i: (b, 0, i)),
        ],
        compiler_params=pltpu.CompilerParams(
            dimension_semantics=("parallel", "parallel"),
            vmem_limit_bytes=V7X_VMEM_LIMIT_BYTES),
        name="dsa_proj",
    )
    qabs, ckv, ckvt, qidx, kidx, widxt = proj(
        x, pre_g.reshape(1, D), wq, wqi, wsm, kv_g.reshape(1, KV_RANK), ki_g.reshape(1, IDX_DIM), wuk)

    far = rel_bias[REL_BUCKETS - 1].astype(F32)
    bias_t = (_bias_toeplitz(rel_bias, 2 * DSA_KC, 3 * DSA_KC, BLOCK) - far[:, None, None]) * LOG2E
    tri = jnp.tril(jnp.ones((DSA_KC, DSA_KC), F32), -1).astype(BF16)

    attn = pl.pallas_call(
        functools.partial(_dsa_attn_body, top_k=top_k),
        out_shape=jax.ShapeDtypeStruct((B, S, D), F32),
        grid=(B, nblk),
        in_specs=[
            pl.BlockSpec((None, BLOCK, D), lambda b, n: (b, n, 0)),
            pl.BlockSpec((None, None, N_HEADS * BLOCK, KV_RANK), lambda b, n: (b, n, 0, 0)),
            pl.BlockSpec((None, None, IDX_HEADS * BLOCK, IDX_DIM), lambda b, n: (b, n, 0, 0)),
            pl.BlockSpec((None, IDX_HEADS, BLOCK), lambda b, n: (b, 0, n)),
            pl.BlockSpec((None, S, IDX_DIM), lambda b, n: (b, 0, 0)),
            pl.BlockSpec((None, S, KV_RANK), lambda b, n: (b, 0, 0)),
            pl.BlockSpec((None, S // DSA_KC, raug, DSA_KC), lambda b, n: (b, 0, 0, 0)),
            _resident((N_HEADS, 3 * DSA_KC, BLOCK)),
            _resident((DSA_KC, DSA_KC)),
            _resident((N_HEADS, HEAD_DIM, KV_RANK)),
            _resident((D, qdim)),
            _resident((1, D)),
        ],
        out_specs=pl.BlockSpec((None, BLOCK, D), lambda b, n: (b, n, 0)),
        scratch_shapes=[
            pltpu.VMEM((S, BLOCK), jnp.int32),
            pltpu.VMEM((S, BLOCK), jnp.int16),
            pltpu.VMEM((S, BLOCK), jnp.int16),
            pltpu.VMEM((N_HEADS // 2, raug, 2 * BLOCK), F32),
            pltpu.VMEM((N_HEADS, 1, BLOCK), F32),
            pltpu.VMEM((qdim, BLOCK), BF16),
            pltpu.VMEM((DSA_KC, BLOCK), F32),
            pltpu.VMEM((2, N_HEADS // 2, DSA_KC, 2 * BLOCK), BF16),
            pltpu.VMEM((2, N_HEADS // 2, 1, 2 * BLOCK), F32),
        ],
        compiler_params=pltpu.CompilerParams(
            dimension_semantics=("parallel", "parallel"),
            vmem_limit_bytes=V7X_VMEM_LIMIT_BYTES),
        name="dsa_attn",
    )
    return attn(x, qabs, qidx, widxt, kidx, ckv, ckvt, bias_t, tri, wuv, wout_t, post_g.reshape(1, D))


def kernel(x, rel_bias, l0_norm_pre_mix, l0_norm_post_mix, l0_attn_w_in, l0_attn_sinks, l0_attn_w_out, l0_norm_pre_ffn, l0_norm_post_ffn, l0_ffn_w_up, l0_ffn_conv_w, l0_ffn_conv_b, l0_ffn_w_down, l1_norm_pre_mix, l1_norm_post_mix, l1_gmlp_w_in, l1_gmlp_ln_g, l1_gmlp_ln_b, l1_gmlp_w_s, l1_gmlp_b_s, l1_gmlp_w_out, l1_norm_pre_ffn, l1_norm_post_ffn, l1_ffn_w_up, l1_ffn_conv_w, l1_ffn_conv_b, l1_ffn_w_down, l2_norm_pre_mix, l2_norm_post_mix, l2_dsa_w_in, l2_dsa_kv_norm, l2_dsa_kidx_norm, l2_dsa_w_uk, l2_dsa_w_uv, l2_dsa_w_out, l2_norm_pre_ffn, l2_norm_post_ffn, l2_ffn_w_up, l2_ffn_conv_w, l2_ffn_conv_b, l2_ffn_w_down, l3_norm_pre_mix, l3_norm_post_mix, l3_attn_w_in, l3_attn_sinks, l3_attn_w_out, l3_norm_pre_ffn, l3_norm_post_ffn, l3_ffn_w_up, l3_ffn_conv_w, l3_ffn_conv_b, l3_ffn_w_down):
    x = _swa(x, l0_norm_pre_mix, l0_norm_post_mix, l0_attn_w_in, l0_attn_sinks, l0_attn_w_out, rel_bias)
    x = _ffn(x, l0_norm_pre_ffn, l0_norm_post_ffn, l0_ffn_w_up, l0_ffn_conv_w, l0_ffn_conv_b, l0_ffn_w_down)
    x = _gmlp(x, l1_norm_pre_mix, l1_norm_post_mix, l1_gmlp_w_in, l1_gmlp_ln_g, l1_gmlp_ln_b, l1_gmlp_w_s, l1_gmlp_b_s, l1_gmlp_w_out)
    x = _ffn(x, l1_norm_pre_ffn, l1_norm_post_ffn, l1_ffn_w_up, l1_ffn_conv_w, l1_ffn_conv_b, l1_ffn_w_down)
    x = _dsa(x, l2_norm_pre_mix, l2_norm_post_mix, l2_dsa_w_in, l2_dsa_kv_norm, l2_dsa_kidx_norm, l2_dsa_w_uk, l2_dsa_w_uv, l2_dsa_w_out, rel_bias)
    x = _ffn(x, l2_norm_pre_ffn, l2_norm_post_ffn, l2_ffn_w_up, l2_ffn_conv_w, l2_ffn_conv_b, l2_ffn_w_down)
    x = _swa(x, l3_norm_pre_mix, l3_norm_post_mix, l3_attn_w_in, l3_attn_sinks, l3_attn_w_out, rel_bias)
    x = _ffn(x, l3_norm_pre_ffn, l3_norm_post_ffn, l3_ffn_w_up, l3_ffn_conv_w, l3_ffn_conv_b, l3_ffn_w_down)
    return x
```

```python
import functools
import math

import jax
import jax.numpy as jnp
from jax import lax
from jax.experimental import pallas as pl
from jax.experimental.pallas import tpu as pltpu

F32 = jnp.float32
BF16 = jnp.bfloat16

EPS = 1e-6
N_HEADS = 16
HEAD_DIM = 64
N_KV_HEADS = 4
GQA_GROUP = N_HEADS // N_KV_HEADS
WINDOW = 128
BLOCK = 128
GMLP_CHUNK = 128
GMLP_GROUPS = 8
KV_RANK = 128
IDX_HEADS = 8
IDX_DIM = 64
TOPK_MAX = 256
REL_BUCKETS = 32
REL_MAX_DIST = 128
CONV_WIDTH = 3

V7X_LANES = 128
V7X_BF16_SUBLANES = 16
V7X_VMEM_LIMIT_BYTES = 56 * 1024 * 1024

LOG2E = math.log2(math.e)


def _rms(xf, g):
    ms = jnp.mean(xf * xf, axis=-1, keepdims=True)
    return xf * lax.rsqrt(ms + EPS) * g


def _resident(shape):
    nd = len(shape)
    return pl.BlockSpec(shape, lambda *_: (0,) * nd, pipeline_mode=pl.Buffered(1))


FFN_TM = 512
FFN_TN = 256
FFN_HALO = V7X_BF16_SUBLANES


def _ffn_body(x_ref, xh_ref, pre_ref, post_ref, wg_ref, wv_ref, cw_ref, cb_ref, wd_ref,
              o_ref, xn_sc, h_sc, a_sc, acc_sc, *, tm, nch, halo):
    i = pl.program_id(1)
    pre = pre_ref[...]
    xn_sc[halo:, :] = _rms(x_ref[...], pre).astype(BF16)
    xh = jnp.where(i > 0, _rms(xh_ref[...], pre), 0.0)
    xn_sc[:halo, :] = xh.astype(BF16)
    acc_sc[...] = jnp.zeros_like(acc_sc)

    def up(j, slot):
        xn = xn_sc[...]
        h_sc[slot, 0] = jnp.dot(xn, wg_ref[j], preferred_element_type=F32)
        h_sc[slot, 1] = jnp.dot(xn, wv_ref[j], preferred_element_type=F32)

    def conv(slot, part, w, b):
        h0 = h_sc[slot, part, halo:, :]
        h1 = h_sc[slot, part, halo - 1:halo - 1 + tm, :]
        h2 = h_sc[slot, part, halo - 2:halo - 2 + tm, :]
        return h0 * w[2:3, :] + h1 * w[1:2, :] + h2 * w[0:1, :] + b

    def act(j, slot):
        g = conv(slot, 0, cw_ref[j], cb_ref[j])
        v = conv(slot, 1, cw_ref[nch + j], cb_ref[nch + j])
        a_sc[slot] = (g * (1.0 / (1.0 + jnp.exp(-g))) * v).astype(BF16)

    def down(j, slot):
        acc_sc[...] += jnp.dot(a_sc[slot], wd_ref[j], preferred_element_type=F32)

    for j in range(nch + 2):
        if j >= 2:
            down(j - 2, j % 2)
        if 1 <= j <= nch:
            act(j - 1, (j - 1) % 2)
        if j < nch:
            up(j, j % 2)
    o_ref[...] = x_ref[...] + _rms(acc_sc[...], post_ref[...])


def _ffn(x, pre_g, post_g, w_up, conv_w, conv_b, w_down):
    B, S, D = x.shape
    dff = w_down.shape[0]
    tm, tn, halo = min(FFN_TM, S), FFN_TN, FFN_HALO
    nch = dff // tn
    assert dff % tn == 0 and S % tm == 0 and tm % halo == 0
    wup = w_up.astype(BF16).reshape(D, 2, nch, tn)
    wg = jnp.transpose(wup[:, 0], (1, 0, 2))
    wv = jnp.transpose(wup[:, 1], (1, 0, 2))
    wd = w_down.astype(BF16).reshape(nch, tn, D)
    cw = jnp.transpose(conv_w.reshape(CONV_WIDTH, 2 * nch, tn), (1, 0, 2))
    cb = conv_b.reshape(2 * nch, 1, tn)
    body = functools.partial(_ffn_body, tm=tm, nch=nch, halo=halo)
    return pl.pallas_call(
        body,
        out_shape=jax.ShapeDtypeStruct((B, S, D), F32),
        grid=(B, S // tm),
        in_specs=[
            pl.BlockSpec((None, tm, D), lambda b, i: (b, i, 0)),
            pl.BlockSpec((None, halo, D), lambda b, i: (b, jnp.maximum(i * (tm // halo) - 1, 0), 0)),
            _resident((1, D)), _resident((1, D)),
            _resident((nch, D, tn)), _resident((nch, D, tn)),
            _resident((2 * nch, CONV_WIDTH, tn)), _resident((2 * nch, 1, tn)),
            _resident((nch, tn, D)),
        ],
        out_specs=pl.BlockSpec((None, tm, D), lambda b, i: (b, i, 0)),
        scratch_shapes=[
            pltpu.VMEM((halo + tm, D), BF16),
            pltpu.VMEM((2, 2, halo + tm, tn), F32),
            pltpu.VMEM((2, tm, tn), BF16),
            pltpu.VMEM((tm, D), F32),
        ],
        compiler_params=pltpu.CompilerParams(
            dimension_semantics=("parallel", "parallel"),
            vmem_limit_bytes=V7X_VMEM_LIMIT_BYTES,
            ),
        name="ffn",
    )(x, x, pre_g.reshape(1, D), post_g.reshape(1, D), wg, wv, cw, cb, wd)


def _rel_bucket(dist):
    max_exact = REL_BUCKETS // 2
    d = jnp.maximum(dist, 0)
    df = jnp.maximum(d, 1).astype(F32)
    large = max_exact + (jnp.log(df / max_exact) / math.log(REL_MAX_DIST / max_exact)
                         * (REL_BUCKETS - max_exact)).astype(jnp.int32)
    large = jnp.minimum(large, REL_BUCKETS - 1)
    return jnp.where(d < max_exact, d, large)


def _bias_lookup(rel_bias, dist):
    onehot = _rel_bucket(dist)[None, ..., None] == jnp.arange(REL_BUCKETS)
    table = jnp.transpose(rel_bias.astype(F32)).reshape((rel_bias.shape[1],) + (1,) * dist.ndim + (REL_BUCKETS,))
    return jnp.sum(jnp.where(onehot, table, 0.0), axis=-1)


def _bias_toeplitz(rel_bias, base, rows, cols):
    n = rows + cols
    vals = _bias_lookup(rel_bias, jnp.arange(base - rows + 1, base + cols + 1))
    h = vals.shape[0]
    skew = jnp.broadcast_to(vals[:, None, :], (h, rows, n)).reshape(h, rows * n)[:, :rows * (n - 1)]
    return skew.reshape(h, rows, n - 1)[:, :, rows - 1:rows - 1 + cols]


SWA_TQ = 256


def _swa_body(x_ref, pre_ref, post_ref, win_ref, sink_ref, bias_ref, wout_ref, o_ref,
              kv_sc, o_sc, s_sc, p_sc, *, tq):
    i = pl.program_id(1)
    qdim = N_HEADS * HEAD_DIM
    kdim = N_KV_HEADS * HEAD_DIM

    @pl.when(i == 0)
    def _():
        kv_sc[0] = jnp.zeros(kv_sc.shape[1:], BF16)

    x = x_ref[...]
    xn = _rms(x, pre_ref[...]).astype(BF16)
    qkv = jnp.dot(xn, win_ref[...], preferred_element_type=F32)
    q = (qkv[:, :qdim] * (HEAD_DIM ** -0.5 * LOG2E)).astype(BF16)
    kv = qkv[:, qdim:].astype(BF16)
    col = lax.broadcasted_iota(jnp.int32, (BLOCK, 2 * BLOCK), 1)
    first_masked = jnp.where(i == 0, BLOCK, 0)
    for r in range(tq // BLOCK):
        rows = slice(r * BLOCK, (r + 1) * BLOCK)
        kv_prev = kv_sc[i % 2] if r == 0 else kv[(r - 1) * BLOCK:r * BLOCK]
        kvw = jnp.concatenate([kv_prev, kv[rows]], axis=0)
        for h in range(N_HEADS):
            kvh = h // GQA_GROUP
            s = lax.dot_general(q[rows, h * HEAD_DIM:(h + 1) * HEAD_DIM],
                                kvw[:, kvh * HEAD_DIM:(kvh + 1) * HEAD_DIM],
                                (((1,), (1,)), ((), ())), preferred_element_type=F32) + bias_ref[h]
            if r == 0:
                s = jnp.where(col < first_masked, -jnp.inf, s)
            s_sc[h] = s
        for h in range(N_HEADS):
            s = s_sc[h]
            sink = sink_ref[h]
            m = jnp.maximum(jnp.max(s, axis=-1, keepdims=True), sink)
            p = jnp.exp2(s - m)
            den = jnp.sum(p, axis=-1, keepdims=True) + jnp.exp2(sink - m)
            p_sc[h] = (p * (1.0 / den)).astype(BF16)
        for h in range(N_HEADS):
            kvh = h // GQA_GROUP
            oh = jnp.dot(p_sc[h], kvw[:, kdim + kvh * HEAD_DIM:kdim + (kvh + 1) * HEAD_DIM],
                         preferred_element_type=F32)
            o_sc[rows, h * HEAD_DIM:(h + 1) * HEAD_DIM] = oh.astype(BF16)
    kv_sc[(i + 1) % 2] = kv[tq - BLOCK:]
    mo = jnp.dot(o_sc[...], wout_ref[...], preferred_element_type=F32)
    o_ref[...] = x + _rms(mo, post_ref[...])


def _swa(x, pre_g, post_g, w_in, sinks, w_out, rel_bias):
    B, S, D = x.shape
    tq = min(SWA_TQ, S)
    qdim, kdim = N_HEADS * HEAD_DIM, N_KV_HEADS * HEAD_DIM
    assert S % tq == 0 and tq % BLOCK == 0
    dist = (jnp.arange(BLOCK)[:, None] + BLOCK) - jnp.arange(2 * BLOCK)[None, :]
    band = (dist >= 0) & (dist < WINDOW)
    bias = jnp.transpose(_bias_toeplitz(rel_bias, BLOCK, 2 * BLOCK, BLOCK), (0, 2, 1))
    bias = jnp.where(band[None], bias * LOG2E, -jnp.inf)
    body = functools.partial(_swa_body, tq=tq)
    return pl.pallas_call(
        body,
        out_shape=jax.ShapeDtypeStruct((B, S, D), F32),
        grid=(B, S // tq),
        in_specs=[
            pl.BlockSpec((None, tq, D), lambda b, i: (b, i, 0)),
            _resident((1, D)), _resident((1, D)),
            _resident((D, qdim + 2 * kdim)),
            pl.BlockSpec(memory_space=pltpu.SMEM),
            _resident((N_HEADS, BLOCK, 2 * BLOCK)),
            _resident((qdim, D)),
        ],
        out_specs=pl.BlockSpec((None, tq, D), lambda b, i: (b, i, 0)),
        scratch_shapes=[
            pltpu.VMEM((2, BLOCK, 2 * kdim), BF16),
            pltpu.VMEM((tq, qdim), BF16),
            pltpu.VMEM((N_HEADS, BLOCK, 2 * BLOCK), F32),
            pltpu.VMEM((N_HEADS, BLOCK, 2 * BLOCK), BF16),
        ],
        compiler_params=pltpu.CompilerParams(
            dimension_semantics=("arbitrary", "arbitrary"),
            vmem_limit_bytes=V7X_VMEM_LIMIT_BYTES),
        name="swa",
    )(x, pre_g.reshape(1, D), post_g.reshape(1, D), w_in.astype(BF16), sinks.astype(F32) * LOG2E,
      bias, w_out.astype(BF16))


GMLP_TQ = 256


def _gmlp_body(x_ref, pre_ref, post_ref, win_ref, lng_ref, lnb_ref, ws_ref, bs_ref, wout_ref,
               o_ref, gated_sc, *, tq):
    x = x_ref[...]
    width = wout_ref.shape[0]
    gdim = width // GMLP_GROUPS
    xn = _rms(x, pre_ref[...]).astype(BF16)
    h = jax.nn.gelu(jnp.dot(xn, win_ref[...], preferred_element_type=F32))
    u, v = h[:, :width], h[:, width:]
    mu = jnp.mean(v, axis=-1, keepdims=True)
    var = jnp.mean(jnp.square(v - mu), axis=-1, keepdims=True)
    vn = ((v - mu) * lax.rsqrt(var + EPS) * lng_ref[...] + lnb_ref[...]).astype(BF16)
    for c in range(tq // GMLP_CHUNK):
        rows = slice(c * GMLP_CHUNK, (c + 1) * GMLP_CHUNK)
        for g in range(GMLP_GROUPS):
            cols = slice(g * gdim, (g + 1) * gdim)
            mixed = jnp.dot(ws_ref[g], vn[rows, cols], preferred_element_type=F32) + bs_ref[g]
            gated_sc[rows, cols] = (u[rows, cols] * mixed).astype(BF16)
    mo = jnp.dot(gated_sc[...], wout_ref[...], preferred_element_type=F32)
    o_ref[...] = x + _rms(mo, post_ref[...])


def _gmlp(x, pre_g, post_g, w_in, ln_g, ln_b, w_s, b_s, w_out):
    B, S, D = x.shape
    width = w_out.shape[0]
    gdim = width // GMLP_GROUPS
    tq = min(GMLP_TQ, S)
    assert S % tq == 0 and tq % GMLP_CHUNK == 0 and gdim == V7X_LANES
    causal = jnp.tril(jnp.ones((GMLP_CHUNK, GMLP_CHUNK), dtype=bool))
    ws = jnp.where(causal, w_s, jnp.zeros_like(w_s)).astype(BF16)
    bs = jnp.broadcast_to(b_s[:, :, None], (GMLP_GROUPS, GMLP_CHUNK, gdim)).astype(F32)
    body = functools.partial(_gmlp_body, tq=tq)
    return pl.pallas_call(
        body,
        out_shape=jax.ShapeDtypeStruct((B, S, D), F32),
        grid=(B, S // tq),
        in_specs=[
            pl.BlockSpec((None, tq, D), lambda b, i: (b, i, 0)),
            _resident((1, D)), _resident((1, D)),
            _resident((D, 2 * width)),
            _resident((1, width)), _resident((1, width)),
            _resident((GMLP_GROUPS, GMLP_CHUNK, GMLP_CHUNK)),
            _resident((GMLP_GROUPS, GMLP_CHUNK, gdim)),
            _resident((width, D)),
        ],
        out_specs=pl.BlockSpec((None, tq, D), lambda b, i: (b, i, 0)),
        scratch_shapes=[pltpu.VMEM((tq, width), BF16)],
        compiler_params=pltpu.CompilerParams(
            dimension_semantics=("parallel", "parallel"),
            vmem_limit_bytes=V7X_VMEM_LIMIT_BYTES),
        name="gmlp",
    )(x, pre_g.reshape(1, D), post_g.reshape(1, D), w_in.astype(BF16),
      ln_g.reshape(1, width), ln_b.reshape(1, width), ws, bs, w_out.astype(BF16))


DSA_TQ = 256
DSA_KC = 256
DSA_SMALL = 256
DSA_AUG = 16
INT_MIN = -2 ** 31
NEG_BIG = -1e30
COUNT_LANES = 4


def _dsa_proj_body(x_ref, pre_ref, wq_ref, wqi_ref, wsm_ref, kvg_ref, kig_ref, wuk_ref,
                   qabs_ref, ckv_ref, ckvt_ref, qidx_ref, kidx_ref, widxt_ref, *, tq):
    xn = _rms(x_ref[...], pre_ref[...]).astype(BF16)
    q = jnp.dot(xn, wq_ref[...], preferred_element_type=F32).astype(BF16)
    for h in range(N_HEADS):
        qa = jnp.dot(q[:, h * HEAD_DIM:(h + 1) * HEAD_DIM], wuk_ref[h],
                     preferred_element_type=F32) * (HEAD_DIM ** -0.5 * LOG2E)
        qa = qa.astype(BF16)
        for blk in range(tq // BLOCK):
            qabs_ref[blk, h * BLOCK:(h + 1) * BLOCK, :] = qa[blk * BLOCK:(blk + 1) * BLOCK]
    qi = jnp.dot(xn, wqi_ref[...], preferred_element_type=F32).astype(BF16)
    for h in range(IDX_HEADS):
        for blk in range(tq // BLOCK):
            qidx_ref[blk, h * BLOCK:(h + 1) * BLOCK, :] = qi[blk * BLOCK:(blk + 1) * BLOCK,
                                                              h * IDX_DIM:(h + 1) * IDX_DIM]
    sm = jnp.dot(xn, wsm_ref[...], preferred_element_type=F32)
    ckv = _rms(sm[:, :KV_RANK], kvg_ref[...])
    ckv_ref[...] = ckv.astype(BF16)
    ones_row = jnp.where(lax.broadcasted_iota(jnp.int32, (DSA_AUG, DSA_KC), 0) == 0,
                         jnp.float32(1.0), jnp.float32(0.0)).astype(BF16)
    for cc in range(tq // DSA_KC):
        ckvt_ref[cc, :KV_RANK, :] = jnp.transpose(ckv[cc * DSA_KC:(cc + 1) * DSA_KC]).astype(BF16)
        ckvt_ref[cc, KV_RANK:, :] = ones_row
    kidx_ref[...] = _rms(sm[:, KV_RANK:KV_RANK + IDX_DIM], kig_ref[...]).astype(BF16)
    tail = jnp.transpose(sm[:, KV_RANK:2 * KV_RANK])
    widxt_ref[...] = tail[IDX_DIM:IDX_DIM + IDX_HEADS, :] * (IDX_HEADS ** -0.5 * IDX_DIM ** -0.5)


def _dsa_attn_body(x_ref, qabs_ref, qidx_ref, widxt_ref, kidx_ref, ckv_ref, ckvt_ref, bias_ref,
                   tri_ref, wuv_ref, wout_ref, post_ref, o_ref,
                   keys_sc, hi_sc, lo_sc, acc_sc, m_sc, o_sc, am_sc, p_sc, alpha_sc, *, top_k):
    n = pl.program_id(1)
    kc = DSA_KC
    nch = (n + 2) // 2
    lane = lax.broadcasted_iota(jnp.int32, (kc, BLOCK), 1)
    row = lax.broadcasted_iota(jnp.int32, (kc, BLOCK), 0)
    qpos = n * BLOCK + lane
    nt = (((1,), (1,)), ((), ()))

    def p1(c, carry):
        base = pl.multiple_of(c * kc, kc)
        xt = lax.dot_general(kidx_ref[pl.ds(base, kc), :], qidx_ref[...], nt,
                             preferred_element_type=F32)
        acc = jnp.zeros((kc, BLOCK), F32)
        for h in range(IDX_HEADS):
            acc = acc + widxt_ref[h:h + 1, :] * jnp.maximum(xt[:, h * BLOCK:(h + 1) * BLOCK], 0.0)
        bits = pltpu.bitcast(acc, jnp.int32)
        key = bits ^ ((bits >> 31) & 0x7FFFFFFF)
        key = jnp.where(c * kc + row <= qpos, key, INT_MIN)
        keys_sc[pl.ds(base, kc), :] = key
        hi_sc[pl.ds(base, kc), :] = (key >> 16).astype(jnp.int16)
        return carry

    lax.fori_loop(0, nch, p1, 0)

    def chunks(body, init):
        return lax.fori_loop(0, nch, lambda c, v: body(pl.multiple_of(c * kc, kc), v), init)

    def count_ge16(src_ref, cand):
        c16 = cand.astype(jnp.int16)

        def body(base, cnts):
            v = src_ref[pl.ds(base, kc), :]
            cnts = list(cnts)
            for j in range(kc // 16):
                a = j % COUNT_LANES
                cnts[a] = jnp.where(v[j * 16:(j + 1) * 16, :] >= c16, cnts[a] + jnp.int16(1), cnts[a])
            return tuple(cnts)

        cnts = chunks(body, (jnp.zeros((16, BLOCK), jnp.int16),) * COUNT_LANES)
        total = functools.reduce(lambda u, v: u + v, [c.astype(jnp.int32) for c in cnts])
        return jnp.sum(total, axis=0, keepdims=True, dtype=jnp.int32)

    def kth_largest16(src_ref, kth):
        def bit(it, u):
            cu = u | jnp.left_shift(jnp.int32(1), 15 - it)
            return jnp.where(count_ge16(src_ref, cu - 2 ** 15) >= kth, cu, u)
        return lax.fori_loop(0, 16, bit, jnp.zeros((1, BLOCK), jnp.int32)) - 2 ** 15

    def count_gt32(cand):
        def body(base, cnts):
            k = keys_sc[pl.ds(base, kc), :]
            cnts = list(cnts)
            for j in range(kc // 8):
                a = j % COUNT_LANES
                cnts[a] = jnp.where(k[j * 8:(j + 1) * 8, :] > cand, cnts[a] + 1, cnts[a])
            return tuple(cnts)
        cnts = chunks(body, (jnp.zeros((8, BLOCK), jnp.int32),) * COUNT_LANES)
        return jnp.sum(functools.reduce(lambda u, v: u + v, cnts), axis=0, keepdims=True, dtype=jnp.int32)

    hi = kth_largest16(hi_sc, top_k)
    above = jnp.where(hi >= 2 ** 15 - 1, 0, count_ge16(hi_sc, jnp.minimum(hi + 1, 2 ** 15 - 1)))

    def low_halves(base, carry):
        k = keys_sc[pl.ds(base, kc), :]
        lo = jnp.where((k >> 16) == hi, (k & 0xFFFF) - 2 ** 15, -2 ** 15)
        lo_sc[pl.ds(base, kc), :] = lo.astype(jnp.int16)
        return carry

    chunks(low_halves, 0)
    lo = kth_largest16(lo_sc, top_k - above)
    thr = hi * 2 ** 16 + (lo + 2 ** 15)
    need = (top_k - count_gt32(thr)).astype(F32)

    acc_sc[...] = jnp.zeros_like(acc_sc)
    m_sc[...] = jnp.full_like(m_sc, NEG_BIG)

    def probs(c, slot, eq_seen, far=False):
        base = pl.multiple_of(c * kc, kc)
        key = keys_sc[pl.ds(base, kc), :]
        eq = key == thr
        eqf = jnp.where(eq, jnp.float32(1.0), jnp.float32(0.0))
        rank = jnp.dot(tri_ref[...], eqf.astype(BF16), preferred_element_type=F32) + eq_seen
        sel = jnp.logical_or(key > thr, jnp.logical_and(eq, rank < need))
        sel = jnp.logical_and(sel, key != INT_MIN)
        am_sc[...] = jnp.where(sel, jnp.float32(0.0), jnp.float32(-jnp.inf))
        ckc = ckv_ref[pl.ds(base, kc), :]
        off = pl.multiple_of(2 * kc - jnp.minimum(n * BLOCK - c * kc, 2 * kc), BLOCK)
        for hp in range(N_HEADS // 2):
            lt2 = lax.dot_general(ckc, qabs_ref[hp * 2 * BLOCK:(hp + 1) * 2 * BLOCK, :], nt,
                                  preferred_element_type=F32)
            for j in range(2):
                h = 2 * hp + j
                ls = slice(j * BLOCK, (j + 1) * BLOCK)
                if far:
                    lt = lt2[:, ls] + am_sc[...]
                else:
                    lt = lt2[:, ls] + (am_sc[...] + bias_ref[h, pl.ds(off, kc), :])
                m_old = m_sc[h]
                m_new = jnp.maximum(m_old, jnp.max(lt, axis=0, keepdims=True))
                m_sc[h] = m_new
                alpha_sc[slot, hp, :, ls] = jnp.exp2(m_old - m_new)
                p_sc[slot, hp, :, ls] = jnp.exp2(lt - m_new).astype(BF16)
        return eq_seen + jnp.sum(eqf, axis=0, keepdims=True)

    def accumulate(c, slot):
        ckt = ckvt_ref[c]
        for hp in range(N_HEADS // 2):
            acc_sc[hp] = alpha_sc[slot, hp] * acc_sc[hp] + jnp.dot(
                ckt, p_sc[slot, hp], preferred_element_type=F32)

    def step01(c, eq_seen, far=False):
        accumulate(c - 1, 0)
        return probs(c, 1, eq_seen, far)

    def step10(c, eq_seen, far=False):
        accumulate(c - 1, 1)
        return probs(c, 0, eq_seen, far)

    def pair(far, k, eq_seen):
        return step10(2 * k + 2, step01(2 * k + 1, eq_seen, far), far)

    nfar = jnp.maximum((n - 2) // 2, 0)
    kfar = jnp.maximum((nfar - 1) // 2, 0)
    eq_seen = probs(0, 0, jnp.zeros((1, BLOCK), F32))
    eq_seen = lax.fori_loop(0, kfar, functools.partial(pair, True), eq_seen)
    eq_seen = lax.fori_loop(kfar, (nch - 1) // 2, functools.partial(pair, False), eq_seen)

    @pl.when(nch % 2 == 0)
    def _():
        step01(nch - 1, eq_seen)
        accumulate(nch - 1, 1)

    @pl.when(nch % 2 == 1)
    def _():
        accumulate(nch - 1, 0)

    for hp in range(N_HEADS // 2):
        a = acc_sc[hp]
        ot = (a[:KV_RANK] * (1.0 / a[KV_RANK:KV_RANK + 1])).astype(BF16)
        for j in range(2):
            h = 2 * hp + j
            o_sc[h * HEAD_DIM:(h + 1) * HEAD_DIM, :] = jnp.dot(
                wuv_ref[h], ot[:, j * BLOCK:(j + 1) * BLOCK],
                preferred_element_type=F32).astype(BF16)
    mt = jnp.dot(wout_ref[...], o_sc[...], preferred_element_type=F32)
    o_ref[...] = x_ref[...] + _rms(jnp.transpose(mt), post_ref[...])


def _dsa(x, pre_g, post_g, w_in, kv_g, ki_g, w_uk, w_uv, w_out, rel_bias):
    B, S, D = x.shape
    qdim = N_HEADS * HEAD_DIM
    qidim = IDX_HEADS * IDX_DIM
    tq = min(DSA_TQ, S)
    top_k = min(TOPK_MAX, S // 4)
    assert S % tq == 0 and tq % DSA_KC == 0 and KV_RANK == BLOCK and DSA_KC == 2 * BLOCK
    c0, c1, c2, c3 = qdim, qdim + KV_RANK, qdim + KV_RANK + qidim, qdim + KV_RANK + qidim + IDX_DIM
    wq = w_in[:, :c0].astype(BF16)
    wqi = w_in[:, c1:c2].astype(BF16)
    wsm = jnp.concatenate([w_in[:, c0:c1], w_in[:, c2:c3], w_in[:, c3:]], axis=1)
    wsm = jnp.pad(wsm, ((0, 0), (0, DSA_SMALL - wsm.shape[1]))).astype(BF16)
    wuk = jnp.transpose(w_uk, (1, 2, 0)).astype(BF16)
    wuv = jnp.transpose(w_uv, (1, 2, 0)).astype(BF16)
    wout_t = jnp.transpose(w_out).astype(BF16)
    nblk = S // BLOCK
    raug = KV_RANK + DSA_AUG

    proj = pl.pallas_call(
        functools.partial(_dsa_proj_body, tq=tq),
        out_shape=[
            jax.ShapeDtypeStruct((B, nblk, N_HEADS * BLOCK, KV_RANK), BF16),
            jax.ShapeDtypeStruct((B, S, KV_RANK), BF16),
            jax.ShapeDtypeStruct((B, S // DSA_KC, raug, DSA_KC), BF16),
            jax.ShapeDtypeStruct((B, nblk, IDX_HEADS * BLOCK, IDX_DIM), BF16),
            jax.ShapeDtypeStruct((B, S, IDX_DIM), BF16),
            jax.ShapeDtypeStruct((B, IDX_HEADS, S), F32),
        ],
        grid=(B, S // tq),
        in_specs=[
            pl.BlockSpec((None, tq, D), lambda b, i: (b, i, 0)),
            _resident((1, D)),
            _resident((D, qdim)), _resident((D, qidim)), _resident((D, DSA_SMALL)),
            _resident((1, KV_RANK)), _resident((1, IDX_DIM)),
            _resident((N_HEADS, HEAD_DIM, KV_RANK)),
        ],
        out_specs=[
            pl.BlockSpec((None, tq // BLOCK, N_HEADS * BLOCK, KV_RANK), lambda b, i: (b, i, 0, 0)),
            pl.BlockSpec((None, tq, KV_RANK), lambda b, i: (b, i, 0)),
            pl.BlockSpec((None, tq // DSA_KC, raug, DSA_KC), lambda b, i: (b, i, 0, 0)),
            pl.BlockSpec((None, tq // BLOCK, IDX_HEADS * BLOCK, IDX_DIM), lambda b, i: (b, i, 0, 0)),
            pl.BlockSpec((None, tq, IDX_DIM), lambda b, i: (b, i, 0)),
            pl.BlockSpec((None, IDX_HEADS, tq), lambda b, i: (b, 0, i)),
        ],
        compiler_params=pltpu.CompilerParams(
            dimension_semantics=("parallel", "parallel"),
            vmem_limit_bytes=V7X_VMEM_LIMIT_BYTES),
        name="dsa_proj",
    )
    qabs, ckv, ckvt, qidx, kidx, widxt = proj(
        x, pre_g.reshape(1, D), wq, wqi, wsm, kv_g.reshape(1, KV_RANK), ki_g.reshape(1, IDX_DIM), wuk)

    far = rel_bias[REL_BUCKETS - 1].astype(F32)
    bias_t = (_bias_toeplitz(rel_bias, 2 * DSA_KC, 3 * DSA_KC, BLOCK) - far[:, None, None]) * LOG2E
    tri = jnp.tril(jnp.ones((DSA_KC, DSA_KC), F32), -1).astype(BF16)

    attn = pl.pallas_call(
        functools.partial(_dsa_attn_body, top_k=top_k),
        out_shape=jax.ShapeDtypeStruct((B, S, D), F32),
        grid=(B, nblk),
        in_specs=[
            pl.BlockSpec((None, BLOCK, D), lambda b, n: (b, n, 0)),
            pl.BlockSpec((None, None, N_HEADS * BLOCK, KV_RANK), lambda b, n: (b, n, 0, 0)),
            pl.BlockSpec((None, None, IDX_HEADS * BLOCK, IDX_DIM), lambda b, n: (b, n, 0, 0)),
            pl.BlockSpec((None, IDX_HEADS, BLOCK), lambda b, n: (b, 0, n)),
            pl.BlockSpec((None, S, IDX_DIM), lambda b, n: (b, 0, 0)),
            pl.BlockSpec((None, S, KV_RANK), lambda b, n: (b, 0, 0)),
            pl.BlockSpec((None, S // DSA_KC, raug, DSA_KC), lambda b, n: (b, 0, 0, 0)),
            _resident((N_HEADS, 3 * DSA_KC, BLOCK)),
            _resident((DSA_KC, DSA_KC)),
            _resident((N_HEADS, HEAD_DIM, KV_RANK)),
            _resident((D, qdim)),
            _resident((1, D)),
        ],
        out_specs=pl.BlockSpec((None, BLOCK, D), lambda b, n: (b, n, 0)),
        scratch_shapes=[
            pltpu.VMEM((S, BLOCK), jnp.int32),
            pltpu.VMEM((S, BLOCK), jnp.int16),
            pltpu.VMEM((S, BLOCK), jnp.int16),
            pltpu.VMEM((N_HEADS // 2, raug, 2 * BLOCK), F32),
            pltpu.VMEM((N_HEADS, 1, BLOCK), F32),
            pltpu.VMEM((qdim, BLOCK), BF16),
            pltpu.VMEM((DSA_KC, BLOCK), F32),
            pltpu.VMEM((2, N_HEADS // 2, DSA_KC, 2 * BLOCK), BF16),
            pltpu.VMEM((2, N_HEADS // 2, 1, 2 * BLOCK), F32),
        ],
        compiler_params=pltpu.CompilerParams(
            dimension_semantics=("parallel", "parallel"),
            vmem_limit_bytes=V7X_VMEM_LIMIT_BYTES),
        name="dsa_attn",
    )
    return attn(x, qabs, qidx, widxt, kidx, ckv, ckvt, bias_t, tri, wuv, wout_t, post_g.reshape(1, D))


def kernel(x, rel_bias, l0_norm_pre_mix, l0_norm_post_mix, l0_attn_w_in, l0_attn_sinks, l0_attn_w_out, l0_norm_pre_ffn, l0_norm_post_ffn, l0_ffn_w_up, l0_ffn_conv_w, l0_ffn_conv_b, l0_ffn_w_down, l1_norm_pre_mix, l1_norm_post_mix, l1_gmlp_w_in, l1_gmlp_ln_g, l1_gmlp_ln_b, l1_gmlp_w_s, l1_gmlp_b_s, l1_gmlp_w_out, l1_norm_pre_ffn, l1_norm_post_ffn, l1_ffn_w_up, l1_ffn_conv_w, l1_ffn_conv_b, l1_ffn_w_down, l2_norm_pre_mix, l2_norm_post_mix, l2_dsa_w_in, l2_dsa_kv_norm, l2_dsa_kidx_norm, l2_dsa_w_uk, l2_dsa_w_uv, l2_dsa_w_out, l2_norm_pre_ffn, l2_norm_post_ffn, l2_ffn_w_up, l2_ffn_conv_w, l2_ffn_conv_b, l2_ffn_w_down, l3_norm_pre_mix, l3_norm_post_mix, l3_attn_w_in, l3_attn_sinks, l3_attn_w_out, l3_norm_pre_ffn, l3_norm_post_ffn, l3_ffn_w_up, l3_ffn_conv_w, l3_ffn_conv_b, l3_ffn_w_down):
    x = _swa(x, l0_norm_pre_mix, l0_norm_post_mix, l0_attn_w_in, l0_attn_sinks, l0_attn_w_out, rel_bias)
    x = _ffn(x, l0_norm_pre_ffn, l0_norm_post_ffn, l0_ffn_w_up, l0_ffn_conv_w, l0_ffn_conv_b, l0_ffn_w_down)
    x = _gmlp(x, l1_norm_pre_mix, l1_norm_post_mix, l1_gmlp_w_in, l1_gmlp_ln_g, l1_gmlp_ln_b, l1_gmlp_w_s, l1_gmlp_b_s, l1_gmlp_w_out)
    x = _ffn(x, l1_norm_pre_ffn, l1_norm_post_ffn, l1_ffn_w_up, l1_ffn_conv_w, l1_ffn_conv_b, l1_ffn_w_down)
    x = _dsa(x, l2_norm_pre_mix, l2_norm_post_mix, l2_dsa_w_in, l2_dsa_kv_norm, l2_dsa_kidx_norm, l2_dsa_w_uk, l2_dsa_w_uv, l2_dsa_w_out, rel_bias)
    x = _ffn(x, l2_norm_pre_ffn, l2_norm_post_ffn, l2_ffn_w_up, l2_ffn_conv_w, l2_ffn_conv_b, l2_ffn_w_down)
    x = _swa(x, l3_norm_pre_mix, l3_norm_post_mix, l3_attn_w_in, l3_attn_sinks, l3_attn_w_out, rel_bias)
    x = _ffn(x, l3_norm_pre_ffn, l3_norm_post_ffn, l3_ffn_w_up, l3_ffn_conv_w, l3_ffn_conv_b, l3_ffn_w_down)
    return x
```

```python
import functools
import math

import jax
import jax.numpy as jnp
from jax import lax
from jax.experimental import pallas as pl
from jax.experimental.pallas import tpu as pltpu

F32 = jnp.float32
BF16 = jnp.bfloat16

EPS = 1e-6
N_HEADS = 16
HEAD_DIM = 64
N_KV_HEADS = 4
GQA_GROUP = N_HEADS // N_KV_HEADS
WINDOW = 128
BLOCK = 128
GMLP_CHUNK = 128
GMLP_GROUPS = 8
KV_RANK = 128
IDX_HEADS = 8
IDX_DIM = 64
TOPK_MAX = 256
REL_BUCKETS = 32
REL_MAX_DIST = 128
CONV_WIDTH = 3

V7X_LANES = 128
V7X_BF16_SUBLANES = 16
V7X_VMEM_LIMIT_BYTES = 56 * 1024 * 1024

LOG2E = math.log2(math.e)


def _rms(xf, g):
    ms = jnp.mean(xf * xf, axis=-1, keepdims=True)
    return xf * lax.rsqrt(ms + EPS) * g


def _resident(shape):
    nd = len(shape)
    return pl.BlockSpec(shape, lambda *_: (0,) * nd, pipeline_mode=pl.Buffered(1))


FFN_TM = 512
FFN_TN = 256
FFN_HALO = V7X_BF16_SUBLANES


def _ffn_body(x_ref, xh_ref, pre_ref, post_ref, wg_ref, wv_ref, cw_ref, cb_ref, wd_ref,
              o_ref, xn_sc, h_sc, a_sc, acc_sc, *, tm, nch, halo):
    i = pl.program_id(1)
    pre = pre_ref[...]
    xn_sc[halo:, :] = _rms(x_ref[...], pre).astype(BF16)
    xh = jnp.where(i > 0, _rms(xh_ref[...], pre), 0.0)
    xn_sc[:halo, :] = xh.astype(BF16)
    acc_sc[...] = jnp.zeros_like(acc_sc)

    def up(j, slot):
        xn = xn_sc[...]
        h_sc[slot, 0] = jnp.dot(xn, wg_ref[j], preferred_element_type=F32)
        h_sc[slot, 1] = jnp.dot(xn, wv_ref[j], preferred_element_type=F32)

    def conv(slot, part, w, b):
        h0 = h_sc[slot, part, halo:, :]
        h1 = h_sc[slot, part, halo - 1:halo - 1 + tm, :]
        h2 = h_sc[slot, part, halo - 2:halo - 2 + tm, :]
        return h0 * w[2:3, :] + h1 * w[1:2, :] + h2 * w[0:1, :] + b

    def act(j, slot):
        g = conv(slot, 0, cw_ref[j], cb_ref[j])
        v = conv(slot, 1, cw_ref[nch + j], cb_ref[nch + j])
        a_sc[slot] = (g * (1.0 / (1.0 + jnp.exp(-g))) * v).astype(BF16)

    def down(j, slot):
        acc_sc[...] += jnp.dot(a_sc[slot], wd_ref[j], preferred_element_type=F32)

    for j in range(nch + 2):
        if j >= 2:
            down(j - 2, j % 2)
        if 1 <= j <= nch:
            act(j - 1, (j - 1) % 2)
        if j < nch:
            up(j, j % 2)
    o_ref[...] = x_ref[...] + _rms(acc_sc[...], post_ref[...])


def _ffn(x, pre_g, post_g, w_up, conv_w, conv_b, w_down):
    B, S, D = x.shape
    dff = w_down.shape[0]
    tm, tn, halo = min(FFN_TM, S), FFN_TN, FFN_HALO
    nch = dff // tn
    assert dff % tn == 0 and S % tm == 0 and tm % halo == 0
    wup = w_up.astype(BF16).reshape(D, 2, nch, tn)
    wg = jnp.transpose(wup[:, 0], (1, 0, 2))
    wv = jnp.transpose(wup[:, 1], (1, 0, 2))
    wd = w_down.astype(BF16).reshape(nch, tn, D)
    cw = jnp.transpose(conv_w.reshape(CONV_WIDTH, 2 * nch, tn), (1, 0, 2))
    cb = conv_b.reshape(2 * nch, 1, tn)
    body = functools.partial(_ffn_body, tm=tm, nch=nch, halo=halo)
    return pl.pallas_call(
        body,
        out_shape=jax.ShapeDtypeStruct((B, S, D), F32),
        grid=(B, S // tm),
        in_specs=[
            pl.BlockSpec((None, tm, D), lambda b, i: (b, i, 0)),
            pl.BlockSpec((None, halo, D), lambda b, i: (b, jnp.maximum(i * (tm // halo) - 1, 0), 0)),
            _resident((1, D)), _resident((1, D)),
            _resident((nch, D, tn)), _resident((nch, D, tn)),
            _resident((2 * nch, CONV_WIDTH, tn)), _resident((2 * nch, 1, tn)),
            _resident((nch, tn, D)),
        ],
        out_specs=pl.BlockSpec((None, tm, D), lambda b, i: (b, i, 0)),
        scratch_shapes=[
            pltpu.VMEM((halo + tm, D), BF16),
            pltpu.VMEM((2, 2, halo + tm, tn), F32),
            pltpu.VMEM((2, tm, tn), BF16),
            pltpu.VMEM((tm, D), F32),
        ],
        compiler_params=pltpu.CompilerParams(
            dimension_semantics=("parallel", "parallel"),
            vmem_limit_bytes=V7X_VMEM_LIMIT_BYTES,
            ),
        name="ffn",
    )(x, x, pre_g.reshape(1, D), post_g.reshape(1, D), wg, wv, cw, cb, wd)


def _rel_bucket(dist):
    max_exact = REL_BUCKETS // 2
    d = jnp.maximum(dist, 0)
    df = jnp.maximum(d, 1).astype(F32)
    large = max_exact + (jnp.log(df / max_exact) / math.log(REL_MAX_DIST / max_exact)
                         * (REL_BUCKETS - max_exact)).astype(jnp.int32)
    large = jnp.minimum(large, REL_BUCKETS - 1)
    return jnp.where(d < max_exact, d, large)


def _bias_lookup(rel_bias, dist):
    onehot = _rel_bucket(dist)[None, ..., None] == jnp.arange(REL_BUCKETS)
    table = jnp.transpose(rel_bias.astype(F32)).reshape((rel_bias.shape[1],) + (1,) * dist.ndim + (REL_BUCKETS,))
    return jnp.sum(jnp.where(onehot, table, 0.0), axis=-1)


def _bias_toeplitz(rel_bias, base, rows, cols):
    n = rows + cols
    vals = _bias_lookup(rel_bias, jnp.arange(base - rows + 1, base + cols + 1))
    h = vals.shape[0]
    skew = jnp.broadcast_to(vals[:, None, :], (h, rows, n)).reshape(h, rows * n)[:, :rows * (n - 1)]
    return skew.reshape(h, rows, n - 1)[:, :, rows - 1:rows - 1 + cols]


SWA_TQ = 256


def _swa_body(x_ref, pre_ref, post_ref, win_ref, sink_ref, bias_ref, wout_ref, o_ref,
              kv_sc, o_sc, s_sc, p_sc, *, tq):
    i = pl.program_id(1)
    qdim = N_HEADS * HEAD_DIM
    kdim = N_KV_HEADS * HEAD_DIM

    @pl.when(i == 0)
    def _():
        kv_sc[0] = jnp.zeros(kv_sc.shape[1:], BF16)

    x = x_ref[...]
    xn = _rms(x, pre_ref[...]).astype(BF16)
    qkv = jnp.dot(xn, win_ref[...], preferred_element_type=F32)
    q = (qkv[:, :qdim] * (HEAD_DIM ** -0.5 * LOG2E)).astype(BF16)
    kv = qkv[:, qdim:].astype(BF16)
    col = lax.broadcasted_iota(jnp.int32, (BLOCK, 2 * BLOCK), 1)
    first_masked = jnp.where(i == 0, BLOCK, 0)
    for r in range(tq // BLOCK):
        rows = slice(r * BLOCK, (r + 1) * BLOCK)
        kv_prev = kv_sc[i % 2] if r == 0 else kv[(r - 1) * BLOCK:r * BLOCK]
        kvw = jnp.concatenate([kv_prev, kv[rows]], axis=0)
        for h in range(N_HEADS):
            kvh = h // GQA_GROUP
            s = lax.dot_general(q[rows, h * HEAD_DIM:(h + 1) * HEAD_DIM],
                                kvw[:, kvh * HEAD_DIM:(kvh + 1) * HEAD_DIM],
                                (((1,), (1,)), ((), ())), preferred_element_type=F32) + bias_ref[h]
            if r == 0:
                s = jnp.where(col < first_masked, -jnp.inf, s)
            s_sc[h] = s
        for h in range(N_HEADS):
            s = s_sc[h]
            sink = sink_ref[h]
            m = jnp.maximum(jnp.max(s, axis=-1, keepdims=True), sink)
            p = jnp.exp2(s - m)
            den = jnp.sum(p, axis=-1, keepdims=True) + jnp.exp2(sink - m)
            p_sc[h] = (p * (1.0 / den)).astype(BF16)
        for h in range(N_HEADS):
            kvh = h // GQA_GROUP
            oh = jnp.dot(p_sc[h], kvw[:, kdim + kvh * HEAD_DIM:kdim + (kvh + 1) * HEAD_DIM],
                         preferred_element_type=F32)
            o_sc[rows, h * HEAD_DIM:(h + 1) * HEAD_DIM] = oh.astype(BF16)
    kv_sc[(i + 1) % 2] = kv[tq - BLOCK:]
    mo = jnp.dot(o_sc[...], wout_ref[...], preferred_element_type=F32)
    o_ref[...] = x + _rms(mo, post_ref[...])


def _swa(x, pre_g, post_g, w_in, sinks, w_out, rel_bias):
    B, S, D = x.shape
    tq = min(SWA_TQ, S)
    qdim, kdim = N_HEADS * HEAD_DIM, N_KV_HEADS * HEAD_DIM
    assert S % tq == 0 and tq % BLOCK == 0
    dist = (jnp.arange(BLOCK)[:, None] + BLOCK) - jnp.arange(2 * BLOCK)[None, :]
    band = (dist >= 0) & (dist < WINDOW)
    bias = jnp.transpose(_bias_toeplitz(rel_bias, BLOCK, 2 * BLOCK, BLOCK), (0, 2, 1))
    bias = jnp.where(band[None], bias * LOG2E, -jnp.inf)
    body = functools.partial(_swa_body, tq=tq)
    return pl.pallas_call(
        body,
        out_shape=jax.ShapeDtypeStruct((B, S, D), F32),
        grid=(B, S // tq),
        in_specs=[
            pl.BlockSpec((None, tq, D), lambda b, i: (b, i, 0)),
            _resident((1, D)), _resident((1, D)),
            _resident((D, qdim + 2 * kdim)),
            pl.BlockSpec(memory_space=pltpu.SMEM),
            _resident((N_HEADS, BLOCK, 2 * BLOCK)),
            _resident((qdim, D)),
        ],
        out_specs=pl.BlockSpec((None, tq, D), lambda b, i: (b, i, 0)),
        scratch_shapes=[
            pltpu.VMEM((2, BLOCK, 2 * kdim), BF16),
            pltpu.VMEM((tq, qdim), BF16),
            pltpu.VMEM((N_HEADS, BLOCK, 2 * BLOCK), F32),
            pltpu.VMEM((N_HEADS, BLOCK, 2 * BLOCK), BF16),
        ],
        compiler_params=pltpu.CompilerParams(
            dimension_semantics=("arbitrary", "arbitrary"),
            vmem_limit_bytes=V7X_VMEM_LIMIT_BYTES),
        name="swa",
    )(x, pre_g.reshape(1, D), post_g.reshape(1, D), w_in.astype(BF16), sinks.astype(F32) * LOG2E,
      bias, w_out.astype(BF16))


GMLP_TQ = 256


def _gmlp_body(x_ref, pre_ref, post_ref, win_ref, lng_ref, lnb_ref, ws_ref, bs_ref, wout_ref,
               o_ref, gated_sc, *, tq):
    x = x_ref[...]
    width = wout_ref.shape[0]
    gdim = width // GMLP_GROUPS
    xn = _rms(x, pre_ref[...]).astype(BF16)
    h = jax.nn.gelu(jnp.dot(xn, win_ref[...], preferred_element_type=F32))
    u, v = h[:, :width], h[:, width:]
    mu = jnp.mean(v, axis=-1, keepdims=True)
    var = jnp.mean(jnp.square(v - mu), axis=-1, keepdims=True)
    vn = ((v - mu) * lax.rsqrt(var + EPS) * lng_ref[...] + lnb_ref[...]).astype(BF16)
    for c in range(tq // GMLP_CHUNK):
        rows = slice(c * GMLP_CHUNK, (c + 1) * GMLP_CHUNK)
        for g in range(GMLP_GROUPS):
            cols = slice(g * gdim, (g + 1) * gdim)
            mixed = jnp.dot(ws_ref[g], vn[rows, cols], preferred_element_type=F32) + bs_ref[g]
            gated_sc[rows, cols] = (u[rows, cols] * mixed).astype(BF16)
    mo = jnp.dot(gated_sc[...], wout_ref[...], preferred_element_type=F32)
    o_ref[...] = x + _rms(mo, post_ref[...])


def _gmlp(x, pre_g, post_g, w_in, ln_g, ln_b, w_s, b_s, w_out):
    B, S, D = x.shape
    width = w_out.shape[0]
    gdim = width // GMLP_GROUPS
    tq = min(GMLP_TQ, S)
    assert S % tq == 0 and tq % GMLP_CHUNK == 0 and gdim == V7X_LANES
    causal = jnp.tril(jnp.ones((GMLP_CHUNK, GMLP_CHUNK), dtype=bool))
    ws = jnp.where(causal, w_s, jnp.zeros_like(w_s)).astype(BF16)
    bs = jnp.broadcast_to(b_s[:, :, None], (GMLP_GROUPS, GMLP_CHUNK, gdim)).astype(F32)
    body = functools.partial(_gmlp_body, tq=tq)
    return pl.pallas_call(
        body,
        out_shape=jax.ShapeDtypeStruct((B, S, D), F32),
        grid=(B, S // tq),
        in_specs=[
            pl.BlockSpec((None, tq, D), lambda b, i: (b, i, 0)),
            _resident((1, D)), _resident((1, D)),
            _resident((D, 2 * width)),
            _resident((1, width)), _resident((1, width)),
            _resident((GMLP_GROUPS, GMLP_CHUNK, GMLP_CHUNK)),
            _resident((GMLP_GROUPS, GMLP_CHUNK, gdim)),
            _resident((width, D)),
        ],
        out_specs=pl.BlockSpec((None, tq, D), lambda b, i: (b, i, 0)),
        scratch_shapes=[pltpu.VMEM((tq, width), BF16)],
        compiler_params=pltpu.CompilerParams(
            dimension_semantics=("parallel", "parallel"),
            vmem_limit_bytes=V7X_VMEM_LIMIT_BYTES),
        name="gmlp",
    )(x, pre_g.reshape(1, D), post_g.reshape(1, D), w_in.astype(BF16),
      ln_g.reshape(1, width), ln_b.reshape(1, width), ws, bs, w_out.astype(BF16))


DSA_TQ = 256
DSA_KC = 256
DSA_SMALL = 256
DSA_AUG = 16
INT_MIN = -2 ** 31
NEG_BIG = -1e30
COUNT_LANES = 4


def _dsa_proj_body(x_ref, pre_ref, wq_ref, wqi_ref, wsm_ref, kvg_ref, kig_ref, wuk_ref,
                   qabs_ref, ckv_ref, ckvt_ref, qidx_ref, kidx_ref, widxt_ref, *, tq):
    xn = _rms(x_ref[...], pre_ref[...]).astype(BF16)
    q = jnp.dot(xn, wq_ref[...], preferred_element_type=F32).astype(BF16)
    for h in range(N_HEADS):
        qa = jnp.dot(q[:, h * HEAD_DIM:(h + 1) * HEAD_DIM], wuk_ref[h],
                     preferred_element_type=F32) * (HEAD_DIM ** -0.5 * LOG2E)
        qa = qa.astype(BF16)
        for blk in range(tq // BLOCK):
            qabs_ref[blk, h * BLOCK:(h + 1) * BLOCK, :] = qa[blk * BLOCK:(blk + 1) * BLOCK]
    qi = jnp.dot(xn, wqi_ref[...], preferred_element_type=F32).astype(BF16)
    for h in range(IDX_HEADS):
        for blk in range(tq // BLOCK):
            qidx_ref[blk, h * BLOCK:(h + 1) * BLOCK, :] = qi[blk * BLOCK:(blk + 1) * BLOCK,
                                                              h * IDX_DIM:(h + 1) * IDX_DIM]
    sm = jnp.dot(xn, wsm_ref[...], preferred_element_type=F32)
    ckv = _rms(sm[:, :KV_RANK], kvg_ref[...])
    ckv_ref[...] = ckv.astype(BF16)
    ones_row = jnp.where(lax.broadcasted_iota(jnp.int32, (DSA_AUG, DSA_KC), 0) == 0,
                         jnp.float32(1.0), jnp.float32(0.0)).astype(BF16)
    for cc in range(tq // DSA_KC):
        ckvt_ref[cc, :KV_RANK, :] = jnp.transpose(ckv[cc * DSA_KC:(cc + 1) * DSA_KC]).astype(BF16)
        ckvt_ref[cc, KV_RANK:, :] = ones_row
    kidx_ref[...] = _rms(sm[:, KV_RANK:KV_RANK + IDX_DIM], kig_ref[...]).astype(BF16)
    tail = jnp.transpose(sm[:, KV_RANK:2 * KV_RANK])
    widxt_ref[...] = tail[IDX_DIM:IDX_DIM + IDX_HEADS, :] * (IDX_HEADS ** -0.5 * IDX_DIM ** -0.5)


def _dsa_attn_body(x_ref, qabs_ref, qidx_ref, widxt_ref, kidx_ref, ckv_ref, ckvt_ref, bias_ref,
                   tri_ref, wuv_ref, wout_ref, post_ref, o_ref,
                   keys_sc, hi_sc, lo_sc, acc_sc, m_sc, o_sc, am_sc, p_sc, alpha_sc, xt_sc, *, top_k):
    n = pl.program_id(1)
    kc = DSA_KC
    nch = (n + 2) // 2
    lane = lax.broadcasted_iota(jnp.int32, (kc, BLOCK), 1)
    row = lax.broadcasted_iota(jnp.int32, (kc, BLOCK), 0)
    qpos = n * BLOCK + lane
    nt = (((1,), (1,)), ((), ()))

    def scores(c, slot):
        xt_sc[slot] = lax.dot_general(kidx_ref[pl.ds(pl.multiple_of(c * kc, kc), kc), :], qidx_ref[...], nt,
                                      preferred_element_type=F32)

    def to_keys(c, slot):
        base = pl.multiple_of(c * kc, kc)
        acc = jnp.zeros((kc, BLOCK), F32)
        for h in range(IDX_HEADS):
            acc = acc + widxt_ref[h:h + 1, :] * jnp.maximum(xt_sc[slot, :, h * BLOCK:(h + 1) * BLOCK], 0.0)
        bits = pltpu.bitcast(acc, jnp.int32)
        key = bits ^ ((bits >> 31) & 0x7FFFFFFF)
        key = jnp.where(c * kc + row <= qpos, key, INT_MIN)
        keys_sc[pl.ds(base, kc), :] = key
        hi_sc[pl.ds(base, kc), :] = (key >> 16).astype(jnp.int16)

    def p1_pair(k, carry):
        c = 2 * k + 1
        to_keys(c - 1, 0)
        scores(c, 1)
        to_keys(c, 1)
        scores(c + 1, 0)
        return carry

    scores(0, 0)
    lax.fori_loop(0, (nch - 1) // 2, p1_pair, 0)

    @pl.when(nch % 2 == 0)
    def _():
        to_keys(nch - 2, 0)
        scores(nch - 1, 1)
        to_keys(nch - 1, 1)

    @pl.when(nch % 2 == 1)
    def _():
        to_keys(nch - 1, 0)

    def chunks(body, init):
        return lax.fori_loop(0, nch, lambda c, v: body(pl.multiple_of(c * kc, kc), v), init)

    def count_ge16(src_ref, cand):
        c16 = cand.astype(jnp.int16)

        def body(base, cnts):
            v = src_ref[pl.ds(base, kc), :]
            cnts = list(cnts)
            for j in range(kc // 16):
                a = j % COUNT_LANES
                cnts[a] = jnp.where(v[j * 16:(j + 1) * 16, :] >= c16, cnts[a] + jnp.int16(1), cnts[a])
            return tuple(cnts)

        cnts = chunks(body, (jnp.zeros((16, BLOCK), jnp.int16),) * COUNT_LANES)
        total = functools.reduce(lambda u, v: u + v, [c.astype(jnp.int32) for c in cnts])
        return jnp.sum(total, axis=0, keepdims=True, dtype=jnp.int32)

    def kth_largest16(src_ref, kth):
        def bit(it, u):
            cu = u | jnp.left_shift(jnp.int32(1), 15 - it)
            return jnp.where(count_ge16(src_ref, cu - 2 ** 15) >= kth, cu, u)
        return lax.fori_loop(0, 16, bit, jnp.zeros((1, BLOCK), jnp.int32)) - 2 ** 15

    def count_gt32(cand):
        def body(base, cnts):
            k = keys_sc[pl.ds(base, kc), :]
            cnts = list(cnts)
            for j in range(kc // 8):
                a = j % COUNT_LANES
                cnts[a] = jnp.where(k[j * 8:(j + 1) * 8, :] > cand, cnts[a] + 1, cnts[a])
            return tuple(cnts)
        cnts = chunks(body, (jnp.zeros((8, BLOCK), jnp.int32),) * COUNT_LANES)
        return jnp.sum(functools.reduce(lambda u, v: u + v, cnts), axis=0, keepdims=True, dtype=jnp.int32)

    hi = kth_largest16(hi_sc, top_k)
    above = jnp.where(hi >= 2 ** 15 - 1, 0, count_ge16(hi_sc, jnp.minimum(hi + 1, 2 ** 15 - 1)))

    def low_halves(base, carry):
        k = keys_sc[pl.ds(base, kc), :]
        lo = jnp.where((k >> 16) == hi, (k & 0xFFFF) - 2 ** 15, -2 ** 15)
        lo_sc[pl.ds(base, kc), :] = lo.astype(jnp.int16)
        return carry

    chunks(low_halves, 0)
    lo = kth_largest16(lo_sc, top_k - above)
    thr = hi * 2 ** 16 + (lo + 2 ** 15)
    need = (top_k - count_gt32(thr)).astype(F32)

    acc_sc[...] = jnp.zeros_like(acc_sc)
    m_sc[...] = jnp.full_like(m_sc, NEG_BIG)

    def probs(c, slot, eq_seen, far=False):
        base = pl.multiple_of(c * kc, kc)
        key = keys_sc[pl.ds(base, kc), :]
        eq = key == thr
        eqf = jnp.where(eq, jnp.float32(1.0), jnp.float32(0.0))
        rank = jnp.dot(tri_ref[...], eqf.astype(BF16), preferred_element_type=F32) + eq_seen
        sel = jnp.logical_or(key > thr, jnp.logical_and(eq, rank < need))
        sel = jnp.logical_and(sel, key != INT_MIN)
        am_sc[...] = jnp.where(sel, jnp.float32(0.0), jnp.float32(-jnp.inf))
        ckc = ckv_ref[pl.ds(base, kc), :]
        off = pl.multiple_of(2 * kc - jnp.minimum(n * BLOCK - c * kc, 2 * kc), BLOCK)
        for hp in range(N_HEADS // 2):
            lt2 = lax.dot_general(ckc, qabs_ref[hp * 2 * BLOCK:(hp + 1) * 2 * BLOCK, :], nt,
                                  preferred_element_type=F32)
            for j in range(2):
                h = 2 * hp + j
                ls = slice(j * BLOCK, (j + 1) * BLOCK)
                if far:
                    lt = lt2[:, ls] + am_sc[...]
                else:
                    lt = lt2[:, ls] + (am_sc[...] + bias_ref[h, pl.ds(off, kc), :])
                m_old = m_sc[h]
                m_new = jnp.maximum(m_old, jnp.max(lt, axis=0, keepdims=True))
                m_sc[h] = m_new
                alpha_sc[slot, hp, :, ls] = jnp.exp2(m_old - m_new)
                p_sc[slot, hp, :, ls] = jnp.exp2(lt - m_new).astype(BF16)
        return eq_seen + jnp.sum(eqf, axis=0, keepdims=True)

    def accumulate(c, slot):
        ckt = ckvt_ref[c]
        for hp in range(N_HEADS // 2):
            acc_sc[hp] = alpha_sc[slot, hp] * acc_sc[hp] + jnp.dot(
                ckt, p_sc[slot, hp], preferred_element_type=F32)

    def step01(c, eq_seen, far=False):
        accumulate(c - 1, 0)
        return probs(c, 1, eq_seen, far)

    def step10(c, eq_seen, far=False):
        accumulate(c - 1, 1)
        return probs(c, 0, eq_seen, far)

    def pair(far, k, eq_seen):
        return step10(2 * k + 2, step01(2 * k + 1, eq_seen, far), far)

    nfar = jnp.maximum((n - 2) // 2, 0)
    kfar = jnp.maximum((nfar - 1) // 2, 0)
    eq_seen = probs(0, 0, jnp.zeros((1, BLOCK), F32))
    eq_seen = lax.fori_loop(0, kfar, functools.partial(pair, True), eq_seen)
    eq_seen = lax.fori_loop(kfar, (nch - 1) // 2, functools.partial(pair, False), eq_seen)

    @pl.when(nch % 2 == 0)
    def _():
        step01(nch - 1, eq_seen)
        accumulate(nch - 1, 1)

    @pl.when(nch % 2 == 1)
    def _():
        accumulate(nch - 1, 0)

    for hp in range(N_HEADS // 2):
        a = acc_sc[hp]
        ot = (a[:KV_RANK] * (1.0 / a[KV_RANK:KV_RANK + 1])).astype(BF16)
        for j in range(2):
            h = 2 * hp + j
            o_sc[h * HEAD_DIM:(h + 1) * HEAD_DIM, :] = jnp.dot(
                wuv_ref[h], ot[:, j * BLOCK:(j + 1) * BLOCK],
                preferred_element_type=F32).astype(BF16)
    mt = jnp.dot(wout_ref[...], o_sc[...], preferred_element_type=F32)
    o_ref[...] = x_ref[...] + _rms(jnp.transpose(mt), post_ref[...])


def _dsa(x, pre_g, post_g, w_in, kv_g, ki_g, w_uk, w_uv, w_out, rel_bias):
    B, S, D = x.shape
    qdim = N_HEADS * HEAD_DIM
    qidim = IDX_HEADS * IDX_DIM
    tq = min(DSA_TQ, S)
    top_k = min(TOPK_MAX, S // 4)
    assert S % tq == 0 and tq % DSA_KC == 0 and KV_RANK == BLOCK and DSA_KC == 2 * BLOCK
    c0, c1, c2, c3 = qdim, qdim + KV_RANK, qdim + KV_RANK + qidim, qdim + KV_RANK + qidim + IDX_DIM
    wq = w_in[:, :c0].astype(BF16)
    wqi = w_in[:, c1:c2].astype(BF16)
    wsm = jnp.concatenate([w_in[:, c0:c1], w_in[:, c2:c3], w_in[:, c3:]], axis=1)
    wsm = jnp.pad(wsm, ((0, 0), (0, DSA_SMALL - wsm.shape[1]))).astype(BF16)
    wuk = jnp.transpose(w_uk, (1, 2, 0)).astype(BF16)
    wuv = jnp.transpose(w_uv, (1, 2, 0)).astype(BF16)
    wout_t = jnp.transpose(w_out).astype(BF16)
    nblk = S // BLOCK
    raug = KV_RANK + DSA_AUG

    proj = pl.pallas_call(
        functools.partial(_dsa_proj_body, tq=tq),
        out_shape=[
            jax.ShapeDtypeStruct((B, nblk, N_HEADS * BLOCK, KV_RANK), BF16),
            jax.ShapeDtypeStruct((B, S, KV_RANK), BF16),
            jax.ShapeDtypeStruct((B, S // DSA_KC, raug, DSA_KC), BF16),
            jax.ShapeDtypeStruct((B, nblk, IDX_HEADS * BLOCK, IDX_DIM), BF16),
            jax.ShapeDtypeStruct((B, S, IDX_DIM), BF16),
            jax.ShapeDtypeStruct((B, IDX_HEADS, S), F32),
        ],
        grid=(B, S // tq),
        in_specs=[
            pl.BlockSpec((None, tq, D), lambda b, i: (b, i, 0)),
            _resident((1, D)),
            _resident((D, qdim)), _resident((D, qidim)), _resident((D, DSA_SMALL)),
            _resident((1, KV_RANK)), _resident((1, IDX_DIM)),
            _resident((N_HEADS, HEAD_DIM, KV_RANK)),
        ],
        out_specs=[
            pl.BlockSpec((None, tq // BLOCK, N_HEADS * BLOCK, KV_RANK), lambda b, i: (b, i, 0, 0)),
            pl.BlockSpec((None, tq, KV_RANK), lambda b, i: (b, i, 0)),
            pl.BlockSpec((None, tq // DSA_KC, raug, DSA_KC), lambda b, i: (b, i, 0, 0)),
            pl.BlockSpec((None, tq // BLOCK, IDX_HEADS * BLOCK, IDX_DIM), lambda b, i: (b, i, 0, 0)),
            pl.BlockSpec((None, tq, IDX_DIM), lambda b, i: (b, i, 0)),
            pl.BlockSpec((None, IDX_HEADS, tq), lambda b, i: (b, 0, i)),
        ],
        compiler_params=pltpu.CompilerParams(
            dimension_semantics=("parallel", "parallel"),
            vmem_limit_bytes=V7X_VMEM_LIMIT_BYTES),
        name="dsa_proj",
    )
    qabs, ckv, ckvt, qidx, kidx, widxt = proj(
        x, pre_g.reshape(1, D), wq, wqi, wsm, kv_g.reshape(1, KV_RANK), ki_g.reshape(1, IDX_DIM), wuk)

    far = rel_bias[REL_BUCKETS - 1].astype(F32)
    bias_t = (_bias_toeplitz(rel_bias, 2 * DSA_KC, 3 * DSA_KC, BLOCK) - far[:, None, None]) * LOG2E
    tri = jnp.tril(jnp.ones((DSA_KC, DSA_KC), F32), -1).astype(BF16)

    attn = pl.pallas_call(
        functools.partial(_dsa_attn_body, top_k=top_k),
        out_shape=jax.ShapeDtypeStruct((B, S, D), F32),
        grid=(B, nblk),
        in_specs=[
            pl.BlockSpec((None, BLOCK, D), lambda b, n: (b, n, 0)),
            pl.BlockSpec((None, None, N_HEADS * BLOCK, KV_RANK), lambda b, n: (b, n, 0, 0)),
            pl.BlockSpec((None, None, IDX_HEADS * BLOCK, IDX_DIM), lambda b, n: (b, n, 0, 0)),
            pl.BlockSpec((None, IDX_HEADS, BLOCK), lambda b, n: (b, 0, n)),
            pl.BlockSpec((None, S, IDX_DIM), lambda b, n: (b, 0, 0)),
            pl.BlockSpec((None, S, KV_RANK), lambda b, n: (b, 0, 0)),
            pl.BlockSpec((None, S // DSA_KC, raug, DSA_KC), lambda b, n: (b, 0, 0, 0)),
            _resident((N_HEADS, 3 * DSA_KC, BLOCK)),
            _resident((DSA_KC, DSA_KC)),
            _resident((N_HEADS, HEAD_DIM, KV_RANK)),
            _resident((D, qdim)),
            _resident((1, D)),
        ],
        out_specs=pl.BlockSpec((None, BLOCK, D), lambda b, n: (b, n, 0)),
        scratch_shapes=[
            pltpu.VMEM((S, BLOCK), jnp.int32),
            pltpu.VMEM((S, BLOCK), jnp.int16),
            pltpu.VMEM((S, BLOCK), jnp.int16),
            pltpu.VMEM((N_HEADS // 2, raug, 2 * BLOCK), F32),
            pltpu.VMEM((N_HEADS, 1, BLOCK), F32),
            pltpu.VMEM((qdim, BLOCK), BF16),
            pltpu.VMEM((DSA_KC, BLOCK), F32),
            pltpu.VMEM((2, N_HEADS // 2, DSA_KC, 2 * BLOCK), BF16),
            pltpu.VMEM((2, N_HEADS // 2, 1, 2 * BLOCK), F32),
            pltpu.VMEM((2, DSA_KC, IDX_HEADS * BLOCK), F32),
        ],
        compiler_params=pltpu.CompilerParams(
            dimension_semantics=("parallel", "parallel"),
            vmem_limit_bytes=V7X_VMEM_LIMIT_BYTES),
        name="dsa_attn",
    )
    return attn(x, qabs, qidx, widxt, kidx, ckv, ckvt, bias_t, tri, wuv, wout_t, post_g.reshape(1, D))


def kernel(x, rel_bias, l0_norm_pre_mix, l0_norm_post_mix, l0_attn_w_in, l0_attn_sinks, l0_attn_w_out, l0_norm_pre_ffn, l0_norm_post_ffn, l0_ffn_w_up, l0_ffn_conv_w, l0_ffn_conv_b, l0_ffn_w_down, l1_norm_pre_mix, l1_norm_post_mix, l1_gmlp_w_in, l1_gmlp_ln_g, l1_gmlp_ln_b, l1_gmlp_w_s, l1_gmlp_b_s, l1_gmlp_w_out, l1_norm_pre_ffn, l1_norm_post_ffn, l1_ffn_w_up, l1_ffn_conv_w, l1_ffn_conv_b, l1_ffn_w_down, l2_norm_pre_mix, l2_norm_post_mix, l2_dsa_w_in, l2_dsa_kv_norm, l2_dsa_kidx_norm, l2_dsa_w_uk, l2_dsa_w_uv, l2_dsa_w_out, l2_norm_pre_ffn, l2_norm_post_ffn, l2_ffn_w_up, l2_ffn_conv_w, l2_ffn_conv_b, l2_ffn_w_down, l3_norm_pre_mix, l3_norm_post_mix, l3_attn_w_in, l3_attn_sinks, l3_attn_w_out, l3_norm_pre_ffn, l3_norm_post_ffn, l3_ffn_w_up, l3_ffn_conv_w, l3_ffn_conv_b, l3_ffn_w_down):
    x = _swa(x, l0_norm_pre_mix, l0_norm_post_mix, l0_attn_w_in, l0_attn_sinks, l0_attn_w_out, rel_bias)
    x = _ffn(x, l0_norm_pre_ffn, l0_norm_post_ffn, l0_ffn_w_up, l0_ffn_conv_w, l0_ffn_conv_b, l0_ffn_w_down)
    x = _gmlp(x, l1_norm_pre_mix, l1_norm_post_mix, l1_gmlp_w_in, l1_gmlp_ln_g, l1_gmlp_ln_b, l1_gmlp_w_s, l1_gmlp_b_s, l1_gmlp_w_out)
    x = _ffn(x, l1_norm_pre_ffn, l1_norm_post_ffn, l1_ffn_w_up, l1_ffn_conv_w, l1_ffn_conv_b, l1_ffn_w_down)
    x = _dsa(x, l2_norm_pre_mix, l2_norm_post_mix, l2_dsa_w_in, l2_dsa_kv_norm, l2_dsa_kidx_norm, l2_dsa_w_uk, l2_dsa_w_uv, l2_dsa_w_out, rel_bias)
    x = _ffn(x, l2_norm_pre_ffn, l2_norm_post_ffn, l2_ffn_w_up, l2_ffn_conv_w, l2_ffn_conv_b, l2_ffn_w_down)
    x = _swa(x, l3_norm_pre_mix, l3_norm_post_mix, l3_attn_w_in, l3_attn_sinks, l3_attn_w_out, rel_bias)
    x = _ffn(x, l3_norm_pre_ffn, l3_norm_post_ffn, l3_ffn_w_up, l3_ffn_conv_w, l3_ffn_conv_b, l3_ffn_w_down)
    return x
```

```python
import functools
import math

import jax
import jax.numpy as jnp
from jax import lax
from jax.experimental import pallas as pl
from jax.experimental.pallas import tpu as pltpu

F32 = jnp.float32
BF16 = jnp.bfloat16

EPS = 1e-6
N_HEADS = 16
HEAD_DIM = 64
N_KV_HEADS = 4
GQA_GROUP = N_HEADS // N_KV_HEADS
WINDOW = 128
BLOCK = 128
GMLP_CHUNK = 128
GMLP_GROUPS = 8
KV_RANK = 128
IDX_HEADS = 8
IDX_DIM = 64
TOPK_MAX = 256
REL_BUCKETS = 32
REL_MAX_DIST = 128
CONV_WIDTH = 3

V7X_LANES = 128
V7X_BF16_SUBLANES = 16
V7X_VMEM_LIMIT_BYTES = 56 * 1024 * 1024

LOG2E = math.log2(math.e)


def _rms(xf, g):
    ms = jnp.mean(xf * xf, axis=-1, keepdims=True)
    return xf * lax.rsqrt(ms + EPS) * g


def _resident(shape):
    nd = len(shape)
    return pl.BlockSpec(shape, lambda *_: (0,) * nd, pipeline_mode=pl.Buffered(1))


FFN_TM = 512
FFN_TN = 256
FFN_HALO = V7X_BF16_SUBLANES


def _ffn_body(x_ref, xh_ref, pre_ref, post_ref, wg_ref, wv_ref, cw_ref, cb_ref, wd_ref,
              o_ref, xn_sc, h_sc, a_sc, acc_sc, *, tm, nch, halo):
    i = pl.program_id(1)
    pre = pre_ref[...]
    xn_sc[halo:, :] = _rms(x_ref[...], pre).astype(BF16)
    xh = jnp.where(i > 0, _rms(xh_ref[...], pre), 0.0)
    xn_sc[:halo, :] = xh.astype(BF16)
    acc_sc[...] = jnp.zeros_like(acc_sc)

    def up(j, slot):
        xn = xn_sc[...]
        h_sc[slot, 0] = jnp.dot(xn, wg_ref[j], preferred_element_type=F32)
        h_sc[slot, 1] = jnp.dot(xn, wv_ref[j], preferred_element_type=F32)

    def conv(slot, part, w, b):
        h0 = h_sc[slot, part, halo:, :]
        h1 = h_sc[slot, part, halo - 1:halo - 1 + tm, :]
        h2 = h_sc[slot, part, halo - 2:halo - 2 + tm, :]
        return h0 * w[2:3, :] + h1 * w[1:2, :] + h2 * w[0:1, :] + b

    def act(j, slot):
        g = conv(slot, 0, cw_ref[j], cb_ref[j])
        v = conv(slot, 1, cw_ref[nch + j], cb_ref[nch + j])
        a_sc[slot] = (g * (1.0 / (1.0 + jnp.exp(-g))) * v).astype(BF16)

    def down(j, slot):
        acc_sc[...] += jnp.dot(a_sc[slot], wd_ref[j], preferred_element_type=F32)

    for j in range(nch + 2):
        if j >= 2:
            down(j - 2, j % 2)
        if 1 <= j <= nch:
            act(j - 1, (j - 1) % 2)
        if j < nch:
            up(j, j % 2)
    o_ref[...] = x_ref[...] + _rms(acc_sc[...], post_ref[...])


def _ffn(x, pre_g, post_g, w_up, conv_w, conv_b, w_down):
    B, S, D = x.shape
    dff = w_down.shape[0]
    tm, tn, halo = min(FFN_TM, S), FFN_TN, FFN_HALO
    nch = dff // tn
    assert dff % tn == 0 and S % tm == 0 and tm % halo == 0
    wup = w_up.astype(BF16).reshape(D, 2, nch, tn)
    wg = jnp.transpose(wup[:, 0], (1, 0, 2))
    wv = jnp.transpose(wup[:, 1], (1, 0, 2))
    wd = w_down.astype(BF16).reshape(nch, tn, D)
    cw = jnp.transpose(conv_w.reshape(CONV_WIDTH, 2 * nch, tn), (1, 0, 2))
    cb = conv_b.reshape(2 * nch, 1, tn)
    body = functools.partial(_ffn_body, tm=tm, nch=nch, halo=halo)
    return pl.pallas_call(
        body,
        out_shape=jax.ShapeDtypeStruct((B, S, D), F32),
        grid=(B, S // tm),
        in_specs=[
            pl.BlockSpec((None, tm, D), lambda b, i: (b, i, 0)),
            pl.BlockSpec((None, halo, D), lambda b, i: (b, jnp.maximum(i * (tm // halo) - 1, 0), 0)),
            _resident((1, D)), _resident((1, D)),
            _resident((nch, D, tn)), _resident((nch, D, tn)),
            _resident((2 * nch, CONV_WIDTH, tn)), _resident((2 * nch, 1, tn)),
            _resident((nch, tn, D)),
        ],
        out_specs=pl.BlockSpec((None, tm, D), lambda b, i: (b, i, 0)),
        scratch_shapes=[
            pltpu.VMEM((halo + tm, D), BF16),
            pltpu.VMEM((2, 2, halo + tm, tn), F32),
            pltpu.VMEM((2, tm, tn), BF16),
            pltpu.VMEM((tm, D), F32),
        ],
        compiler_params=pltpu.CompilerParams(
            dimension_semantics=("parallel", "parallel"),
            vmem_limit_bytes=V7X_VMEM_LIMIT_BYTES,
            ),
        name="ffn",
    )(x, x, pre_g.reshape(1, D), post_g.reshape(1, D), wg, wv, cw, cb, wd)


def _rel_bucket(dist):
    max_exact = REL_BUCKETS // 2
    d = jnp.maximum(dist, 0)
    df = jnp.maximum(d, 1).astype(F32)
    large = max_exact + (jnp.log(df / max_exact) / math.log(REL_MAX_DIST / max_exact)
                         * (REL_BUCKETS - max_exact)).astype(jnp.int32)
    large = jnp.minimum(large, REL_BUCKETS - 1)
    return jnp.where(d < max_exact, d, large)


def _bias_lookup(rel_bias, dist):
    onehot = _rel_bucket(dist)[None, ..., None] == jnp.arange(REL_BUCKETS)
    table = jnp.transpose(rel_bias.astype(F32)).reshape((rel_bias.shape[1],) + (1,) * dist.ndim + (REL_BUCKETS,))
    return jnp.sum(jnp.where(onehot, table, 0.0), axis=-1)


def _bias_toeplitz(rel_bias, base, rows, cols):
    n = rows + cols
    vals = _bias_lookup(rel_bias, jnp.arange(base - rows + 1, base + cols + 1))
    h = vals.shape[0]
    skew = jnp.broadcast_to(vals[:, None, :], (h, rows, n)).reshape(h, rows * n)[:, :rows * (n - 1)]
    return skew.reshape(h, rows, n - 1)[:, :, rows - 1:rows - 1 + cols]


SWA_TQ = 512


def _swa_body(x_ref, pre_ref, post_ref, win_ref, sink_ref, bias_ref, wout_ref, o_ref,
              kv_sc, o_sc, s_sc, p_sc, *, tq):
    i = pl.program_id(1)
    qdim = N_HEADS * HEAD_DIM
    kdim = N_KV_HEADS * HEAD_DIM

    @pl.when(i == 0)
    def _():
        kv_sc[0] = jnp.zeros(kv_sc.shape[1:], BF16)

    x = x_ref[...]
    xn = _rms(x, pre_ref[...]).astype(BF16)
    qkv = jnp.dot(xn, win_ref[...], preferred_element_type=F32)
    q = (qkv[:, :qdim] * (HEAD_DIM ** -0.5 * LOG2E)).astype(BF16)
    kv = qkv[:, qdim:].astype(BF16)
    col = lax.broadcasted_iota(jnp.int32, (BLOCK, 2 * BLOCK), 1)
    first_masked = jnp.where(i == 0, BLOCK, 0)
    for r in range(tq // BLOCK):
        rows = slice(r * BLOCK, (r + 1) * BLOCK)
        kv_prev = kv_sc[i % 2] if r == 0 else kv[(r - 1) * BLOCK:r * BLOCK]
        kvw = jnp.concatenate([kv_prev, kv[rows]], axis=0)
        for h in range(N_HEADS):
            kvh = h // GQA_GROUP
            s = lax.dot_general(q[rows, h * HEAD_DIM:(h + 1) * HEAD_DIM],
                                kvw[:, kvh * HEAD_DIM:(kvh + 1) * HEAD_DIM],
                                (((1,), (1,)), ((), ())), preferred_element_type=F32) + bias_ref[h]
            if r == 0:
                s = jnp.where(col < first_masked, -jnp.inf, s)
            s_sc[h] = s
        for h in range(N_HEADS):
            s = s_sc[h]
            sink = sink_ref[h]
            m = jnp.maximum(jnp.max(s, axis=-1, keepdims=True), sink)
            p = jnp.exp2(s - m)
            den = jnp.sum(p, axis=-1, keepdims=True) + jnp.exp2(sink - m)
            p_sc[h] = (p * (1.0 / den)).astype(BF16)
        for h in range(N_HEADS):
            kvh = h // GQA_GROUP
            oh = jnp.dot(p_sc[h], kvw[:, kdim + kvh * HEAD_DIM:kdim + (kvh + 1) * HEAD_DIM],
                         preferred_element_type=F32)
            o_sc[rows, h * HEAD_DIM:(h + 1) * HEAD_DIM] = oh.astype(BF16)
    kv_sc[(i + 1) % 2] = kv[tq - BLOCK:]
    mo = jnp.dot(o_sc[...], wout_ref[...], preferred_element_type=F32)
    o_ref[...] = x + _rms(mo, post_ref[...])


def _swa(x, pre_g, post_g, w_in, sinks, w_out, rel_bias):
    B, S, D = x.shape
    tq = min(SWA_TQ, S)
    qdim, kdim = N_HEADS * HEAD_DIM, N_KV_HEADS * HEAD_DIM
    assert S % tq == 0 and tq % BLOCK == 0
    dist = (jnp.arange(BLOCK)[:, None] + BLOCK) - jnp.arange(2 * BLOCK)[None, :]
    band = (dist >= 0) & (dist < WINDOW)
    bias = jnp.transpose(_bias_toeplitz(rel_bias, BLOCK, 2 * BLOCK, BLOCK), (0, 2, 1))
    bias = jnp.where(band[None], bias * LOG2E, -jnp.inf)
    body = functools.partial(_swa_body, tq=tq)
    return pl.pallas_call(
        body,
        out_shape=jax.ShapeDtypeStruct((B, S, D), F32),
        grid=(B, S // tq),
        in_specs=[
            pl.BlockSpec((None, tq, D), lambda b, i: (b, i, 0)),
            _resident((1, D)), _resident((1, D)),
            _resident((D, qdim + 2 * kdim)),
            pl.BlockSpec(memory_space=pltpu.SMEM),
            _resident((N_HEADS, BLOCK, 2 * BLOCK)),
            _resident((qdim, D)),
        ],
        out_specs=pl.BlockSpec((None, tq, D), lambda b, i: (b, i, 0)),
        scratch_shapes=[
            pltpu.VMEM((2, BLOCK, 2 * kdim), BF16),
            pltpu.VMEM((tq, qdim), BF16),
            pltpu.VMEM((N_HEADS, BLOCK, 2 * BLOCK), F32),
            pltpu.VMEM((N_HEADS, BLOCK, 2 * BLOCK), BF16),
        ],
        compiler_params=pltpu.CompilerParams(
            dimension_semantics=("arbitrary", "arbitrary"),
            vmem_limit_bytes=V7X_VMEM_LIMIT_BYTES),
        name="swa",
    )(x, pre_g.reshape(1, D), post_g.reshape(1, D), w_in.astype(BF16), sinks.astype(F32) * LOG2E,
      bias, w_out.astype(BF16))


GMLP_TQ = 512


def _gmlp_body(x_ref, pre_ref, post_ref, win_ref, lng_ref, lnb_ref, ws_ref, bs_ref, wout_ref,
               o_ref, gated_sc, *, tq):
    x = x_ref[...]
    width = wout_ref.shape[0]
    gdim = width // GMLP_GROUPS
    xn = _rms(x, pre_ref[...]).astype(BF16)
    h = jax.nn.gelu(jnp.dot(xn, win_ref[...], preferred_element_type=F32))
    u, v = h[:, :width], h[:, width:]
    mu = jnp.mean(v, axis=-1, keepdims=True)
    var = jnp.mean(jnp.square(v - mu), axis=-1, keepdims=True)
    vn = ((v - mu) * lax.rsqrt(var + EPS) * lng_ref[...] + lnb_ref[...]).astype(BF16)
    for c in range(tq // GMLP_CHUNK):
        rows = slice(c * GMLP_CHUNK, (c + 1) * GMLP_CHUNK)
        for g in range(GMLP_GROUPS):
            cols = slice(g * gdim, (g + 1) * gdim)
            mixed = jnp.dot(ws_ref[g], vn[rows, cols], preferred_element_type=F32) + bs_ref[g]
            gated_sc[rows, cols] = (u[rows, cols] * mixed).astype(BF16)
    mo = jnp.dot(gated_sc[...], wout_ref[...], preferred_element_type=F32)
    o_ref[...] = x + _rms(mo, post_ref[...])


def _gmlp(x, pre_g, post_g, w_in, ln_g, ln_b, w_s, b_s, w_out):
    B, S, D = x.shape
    width = w_out.shape[0]
    gdim = width // GMLP_GROUPS
    tq = min(GMLP_TQ, S)
    assert S % tq == 0 and tq % GMLP_CHUNK == 0 and gdim == V7X_LANES
    causal = jnp.tril(jnp.ones((GMLP_CHUNK, GMLP_CHUNK), dtype=bool))
    ws = jnp.where(causal, w_s, jnp.zeros_like(w_s)).astype(BF16)
    bs = jnp.broadcast_to(b_s[:, :, None], (GMLP_GROUPS, GMLP_CHUNK, gdim)).astype(F32)
    body = functools.partial(_gmlp_body, tq=tq)
    return pl.pallas_call(
        body,
        out_shape=jax.ShapeDtypeStruct((B, S, D), F32),
        grid=(B, S // tq),
        in_specs=[
            pl.BlockSpec((None, tq, D), lambda b, i: (b, i, 0)),
            _resident((1, D)), _resident((1, D)),
            _resident((D, 2 * width)),
            _resident((1, width)), _resident((1, width)),
            _resident((GMLP_GROUPS, GMLP_CHUNK, GMLP_CHUNK)),
            _resident((GMLP_GROUPS, GMLP_CHUNK, gdim)),
            _resident((width, D)),
        ],
        out_specs=pl.BlockSpec((None, tq, D), lambda b, i: (b, i, 0)),
        scratch_shapes=[pltpu.VMEM((tq, width), BF16)],
        compiler_params=pltpu.CompilerParams(
            dimension_semantics=("parallel", "parallel"),
            vmem_limit_bytes=V7X_VMEM_LIMIT_BYTES),
        name="gmlp",
    )(x, pre_g.reshape(1, D), post_g.reshape(1, D), w_in.astype(BF16),
      ln_g.reshape(1, width), ln_b.reshape(1, width), ws, bs, w_out.astype(BF16))


DSA_TQ = 512
DSA_KC = 256
DSA_SMALL = 256
DSA_AUG = 16
INT_MIN = -2 ** 31
NEG_BIG = -1e30
COUNT_LANES = 4


def _dsa_proj_body(x_ref, pre_ref, wq_ref, wqi_ref, wsm_ref, kvg_ref, kig_ref, wuk_ref,
                   qabs_ref, ckv_ref, ckvt_ref, qidx_ref, kidx_ref, widxt_ref, *, tq):
    xn = _rms(x_ref[...], pre_ref[...]).astype(BF16)
    q = jnp.dot(xn, wq_ref[...], preferred_element_type=F32).astype(BF16)
    for h in range(N_HEADS):
        qa = jnp.dot(q[:, h * HEAD_DIM:(h + 1) * HEAD_DIM], wuk_ref[h],
                     preferred_element_type=F32) * (HEAD_DIM ** -0.5 * LOG2E)
        qa = qa.astype(BF16)
        for blk in range(tq // BLOCK):
            qabs_ref[blk, h * BLOCK:(h + 1) * BLOCK, :] = qa[blk * BLOCK:(blk + 1) * BLOCK]
    qi = jnp.dot(xn, wqi_ref[...], preferred_element_type=F32).astype(BF16)
    for h in range(IDX_HEADS):
        for blk in range(tq // BLOCK):
            qidx_ref[blk, h * BLOCK:(h + 1) * BLOCK, :] = qi[blk * BLOCK:(blk + 1) * BLOCK,
                                                              h * IDX_DIM:(h + 1) * IDX_DIM]
    sm = jnp.dot(xn, wsm_ref[...], preferred_element_type=F32)
    ckv = _rms(sm[:, :KV_RANK], kvg_ref[...])
    ckv_ref[...] = ckv.astype(BF16)
    ones_row = jnp.where(lax.broadcasted_iota(jnp.int32, (DSA_AUG, DSA_KC), 0) == 0,
                         jnp.float32(1.0), jnp.float32(0.0)).astype(BF16)
    for cc in range(tq // DSA_KC):
        ckvt_ref[cc, :KV_RANK, :] = jnp.transpose(ckv[cc * DSA_KC:(cc + 1) * DSA_KC]).astype(BF16)
        ckvt_ref[cc, KV_RANK:, :] = ones_row
    kidx_ref[...] = _rms(sm[:, KV_RANK:KV_RANK + IDX_DIM], kig_ref[...]).astype(BF16)
    tail = jnp.transpose(sm[:, KV_RANK:2 * KV_RANK])
    widxt_ref[...] = tail[IDX_DIM:IDX_DIM + IDX_HEADS, :] * (IDX_HEADS ** -0.5 * IDX_DIM ** -0.5)


def _dsa_attn_body(x_ref, qabs_ref, qidx_ref, widxt_ref, kidx_ref, ckv_ref, ckvt_ref, bias_ref,
                   tri_ref, wuv_ref, wout_ref, post_ref, o_ref,
                   keys_sc, hi_sc, lo_sc, acc_sc, m_sc, o_sc, am_sc, p_sc, alpha_sc, xt_sc, *, top_k):
    n = pl.program_id(1)
    kc = DSA_KC
    nch = (n + 2) // 2
    lane = lax.broadcasted_iota(jnp.int32, (kc, BLOCK), 1)
    row = lax.broadcasted_iota(jnp.int32, (kc, BLOCK), 0)
    qpos = n * BLOCK + lane
    nt = (((1,), (1,)), ((), ()))

    def scores(c, slot):
        xt_sc[slot] = lax.dot_general(kidx_ref[pl.ds(pl.multiple_of(c * kc, kc), kc), :], qidx_ref[...], nt,
                                      preferred_element_type=F32)

    def to_keys(c, slot):
        base = pl.multiple_of(c * kc, kc)
        acc = jnp.zeros((kc, BLOCK), F32)
        for h in range(IDX_HEADS):
            acc = acc + widxt_ref[h:h + 1, :] * jnp.maximum(xt_sc[slot, :, h * BLOCK:(h + 1) * BLOCK], 0.0)
        bits = pltpu.bitcast(acc, jnp.int32)
        key = bits ^ ((bits >> 31) & 0x7FFFFFFF)
        key = jnp.where(c * kc + row <= qpos, key, INT_MIN)
        keys_sc[pl.ds(base, kc), :] = key
        hi_sc[pl.ds(base, kc), :] = (key >> 16).astype(jnp.int16)

    def p1_pair(k, carry):
        c = 2 * k + 1
        to_keys(c - 1, 0)
        scores(c, 1)
        to_keys(c, 1)
        scores(c + 1, 0)
        return carry

    scores(0, 0)
    lax.fori_loop(0, (nch - 1) // 2, p1_pair, 0)

    @pl.when(nch % 2 == 0)
    def _():
        to_keys(nch - 2, 0)
        scores(nch - 1, 1)
        to_keys(nch - 1, 1)

    @pl.when(nch % 2 == 1)
    def _():
        to_keys(nch - 1, 0)

    def chunks(body, init):
        return lax.fori_loop(0, nch, lambda c, v: body(pl.multiple_of(c * kc, kc), v), init)

    def count_ge16(src_ref, cand):
        c16 = cand.astype(jnp.int16)

        def body(base, cnts):
            v = src_ref[pl.ds(base, kc), :]
            cnts = list(cnts)
            for j in range(kc // 16):
                a = j % COUNT_LANES
                cnts[a] = jnp.where(v[j * 16:(j + 1) * 16, :] >= c16, cnts[a] + jnp.int16(1), cnts[a])
            return tuple(cnts)

        cnts = chunks(body, (jnp.zeros((16, BLOCK), jnp.int16),) * COUNT_LANES)
        total = functools.reduce(lambda u, v: u + v, [c.astype(jnp.int32) for c in cnts])
        return jnp.sum(total, axis=0, keepdims=True, dtype=jnp.int32)

    def kth_largest16(src_ref, kth):
        def bit(it, u):
            cu = u | jnp.left_shift(jnp.int32(1), 15 - it)
            return jnp.where(count_ge16(src_ref, cu - 2 ** 15) >= kth, cu, u)
        return lax.fori_loop(0, 16, bit, jnp.zeros((1, BLOCK), jnp.int32)) - 2 ** 15

    def count_gt32(cand):
        def body(base, cnts):
            k = keys_sc[pl.ds(base, kc), :]
            cnts = list(cnts)
            for j in range(kc // 8):
                a = j % COUNT_LANES
                cnts[a] = jnp.where(k[j * 8:(j + 1) * 8, :] > cand, cnts[a] + 1, cnts[a])
            return tuple(cnts)
        cnts = chunks(body, (jnp.zeros((8, BLOCK), jnp.int32),) * COUNT_LANES)
        return jnp.sum(functools.reduce(lambda u, v: u + v, cnts), axis=0, keepdims=True, dtype=jnp.int32)

    hi = kth_largest16(hi_sc, top_k)
    above = jnp.where(hi >= 2 ** 15 - 1, 0, count_ge16(hi_sc, jnp.minimum(hi + 1, 2 ** 15 - 1)))

    def low_halves(base, carry):
        k = keys_sc[pl.ds(base, kc), :]
        lo = jnp.where((k >> 16) == hi, (k & 0xFFFF) - 2 ** 15, -2 ** 15)
        lo_sc[pl.ds(base, kc), :] = lo.astype(jnp.int16)
        return carry

    chunks(low_halves, 0)
    lo = kth_largest16(lo_sc, top_k - above)
    thr = hi * 2 ** 16 + (lo + 2 ** 15)
    need = (top_k - count_gt32(thr)).astype(F32)

    acc_sc[...] = jnp.zeros_like(acc_sc)
    m_sc[...] = jnp.full_like(m_sc, NEG_BIG)

    def probs(c, slot, eq_seen, far=False):
        base = pl.multiple_of(c * kc, kc)
        key = keys_sc[pl.ds(base, kc), :]
        eq = key == thr
        eqf = jnp.where(eq, jnp.float32(1.0), jnp.float32(0.0))
        rank = jnp.dot(tri_ref[...], eqf.astype(BF16), preferred_element_type=F32) + eq_seen
        sel = jnp.logical_or(key > thr, jnp.logical_and(eq, rank < need))
        sel = jnp.logical_and(sel, key != INT_MIN)
        am_sc[...] = jnp.where(sel, jnp.float32(0.0), jnp.float32(-jnp.inf))
        ckc = ckv_ref[pl.ds(base, kc), :]
        off = pl.multiple_of(2 * kc - jnp.minimum(n * BLOCK - c * kc, 2 * kc), BLOCK)
        for hp in range(N_HEADS // 2):
            lt2 = lax.dot_general(ckc, qabs_ref[hp * 2 * BLOCK:(hp + 1) * 2 * BLOCK, :], nt,
                                  preferred_element_type=F32)
            for j in range(2):
                h = 2 * hp + j
                ls = slice(j * BLOCK, (j + 1) * BLOCK)
                if far:
                    lt = lt2[:, ls] + am_sc[...]
                else:
                    lt = lt2[:, ls] + (am_sc[...] + bias_ref[h, pl.ds(off, kc), :])
                m_old = m_sc[h]
                m_new = jnp.maximum(m_old, jnp.max(lt, axis=0, keepdims=True))
                m_sc[h] = m_new
                alpha_sc[slot, hp, :, ls] = jnp.exp2(m_old - m_new)
                p_sc[slot, hp, :, ls] = jnp.exp2(lt - m_new).astype(BF16)
        return eq_seen + jnp.sum(eqf, axis=0, keepdims=True)

    def accumulate(c, slot):
        ckt = ckvt_ref[c]
        for hp in range(N_HEADS // 2):
            acc_sc[hp] = alpha_sc[slot, hp] * acc_sc[hp] + jnp.dot(
                ckt, p_sc[slot, hp], preferred_element_type=F32)

    def step01(c, eq_seen, far=False):
        accumulate(c - 1, 0)
        return probs(c, 1, eq_seen, far)

    def step10(c, eq_seen, far=False):
        accumulate(c - 1, 1)
        return probs(c, 0, eq_seen, far)

    def pair(far, k, eq_seen):
        return step10(2 * k + 2, step01(2 * k + 1, eq_seen, far), far)

    nfar = jnp.maximum((n - 2) // 2, 0)
    kfar = jnp.maximum((nfar - 1) // 2, 0)
    eq_seen = probs(0, 0, jnp.zeros((1, BLOCK), F32))
    eq_seen = lax.fori_loop(0, kfar, functools.partial(pair, True), eq_seen)
    eq_seen = lax.fori_loop(kfar, (nch - 1) // 2, functools.partial(pair, False), eq_seen)

    @pl.when(nch % 2 == 0)
    def _():
        step01(nch - 1, eq_seen)
        accumulate(nch - 1, 1)

    @pl.when(nch % 2 == 1)
    def _():
        accumulate(nch - 1, 0)

    for hp in range(N_HEADS // 2):
        a = acc_sc[hp]
        ot = (a[:KV_RANK] * (1.0 / a[KV_RANK:KV_RANK + 1])).astype(BF16)
        for j in range(2):
            h = 2 * hp + j
            o_sc[h * HEAD_DIM:(h + 1) * HEAD_DIM, :] = jnp.dot(
                wuv_ref[h], ot[:, j * BLOCK:(j + 1) * BLOCK],
                preferred_element_type=F32).astype(BF16)
    mt = jnp.dot(wout_ref[...], o_sc[...], preferred_element_type=F32)
    o_ref[...] = x_ref[...] + _rms(jnp.transpose(mt), post_ref[...])


def _dsa(x, pre_g, post_g, w_in, kv_g, ki_g, w_uk, w_uv, w_out, rel_bias):
    B, S, D = x.shape
    qdim = N_HEADS * HEAD_DIM
    qidim = IDX_HEADS * IDX_DIM
    tq = min(DSA_TQ, S)
    top_k = min(TOPK_MAX, S // 4)
    assert S % tq == 0 and tq % DSA_KC == 0 and KV_RANK == BLOCK and DSA_KC == 2 * BLOCK
    c0, c1, c2, c3 = qdim, qdim + KV_RANK, qdim + KV_RANK + qidim, qdim + KV_RANK + qidim + IDX_DIM
    wq = w_in[:, :c0].astype(BF16)
    wqi = w_in[:, c1:c2].astype(BF16)
    wsm = jnp.concatenate([w_in[:, c0:c1], w_in[:, c2:c3], w_in[:, c3:]], axis=1)
    wsm = jnp.pad(wsm, ((0, 0), (0, DSA_SMALL - wsm.shape[1]))).astype(BF16)
    wuk = jnp.transpose(w_uk, (1, 2, 0)).astype(BF16)
    wuv = jnp.transpose(w_uv, (1, 2, 0)).astype(BF16)
    wout_t = jnp.transpose(w_out).astype(BF16)
    nblk = S // BLOCK
    raug = KV_RANK + DSA_AUG

    proj = pl.pallas_call(
        functools.partial(_dsa_proj_body, tq=tq),
        out_shape=[
            jax.ShapeDtypeStruct((B, nblk, N_HEADS * BLOCK, KV_RANK), BF16),
            jax.ShapeDtypeStruct((B, S, KV_RANK), BF16),
            jax.ShapeDtypeStruct((B, S // DSA_KC, raug, DSA_KC), BF16),
            jax.ShapeDtypeStruct((B, nblk, IDX_HEADS * BLOCK, IDX_DIM), BF16),
            jax.ShapeDtypeStruct((B, S, IDX_DIM), BF16),
            jax.ShapeDtypeStruct((B, IDX_HEADS, S), F32),
        ],
        grid=(B, S // tq),
        in_specs=[
            pl.BlockSpec((None, tq, D), lambda b, i: (b, i, 0)),
            _resident((1, D)),
            _resident((D, qdim)), _resident((D, qidim)), _resident((D, DSA_SMALL)),
            _resident((1, KV_RANK)), _resident((1, IDX_DIM)),
            _resident((N_HEADS, HEAD_DIM, KV_RANK)),
        ],
        out_specs=[
            pl.BlockSpec((None, tq // BLOCK, N_HEADS * BLOCK, KV_RANK), lambda b, i: (b, i, 0, 0)),
            pl.BlockSpec((None, tq, KV_RANK), lambda b, i: (b, i, 0)),
            pl.BlockSpec((None, tq // DSA_KC, raug, DSA_KC), lambda b, i: (b, i, 0, 0)),
            pl.BlockSpec((None, tq // BLOCK, IDX_HEADS * BLOCK, IDX_DIM), lambda b, i: (b, i, 0, 0)),
            pl.BlockSpec((None, tq, IDX_DIM), lambda b, i: (b, i, 0)),
            pl.BlockSpec((None, IDX_HEADS, tq), lambda b, i: (b, 0, i)),
        ],
        compiler_params=pltpu.CompilerParams(
            dimension_semantics=("parallel", "parallel"),
            vmem_limit_bytes=V7X_VMEM_LIMIT_BYTES),
        name="dsa_proj",
    )
    qabs, ckv, ckvt, qidx, kidx, widxt = proj(
        x, pre_g.reshape(1, D), wq, wqi, wsm, kv_g.reshape(1, KV_RANK), ki_g.reshape(1, IDX_DIM), wuk)

    far = rel_bias[REL_BUCKETS - 1].astype(F32)
    bias_t = (_bias_toeplitz(rel_bias, 2 * DSA_KC, 3 * DSA_KC, BLOCK) - far[:, None, None]) * LOG2E
    tri = jnp.tril(jnp.ones((DSA_KC, DSA_KC), F32), -1).astype(BF16)

    attn = pl.pallas_call(
        functools.partial(_dsa_attn_body, top_k=top_k),
        out_shape=jax.ShapeDtypeStruct((B, S, D), F32),
        grid=(B, nblk),
        in_specs=[
            pl.BlockSpec((None, BLOCK, D), lambda b, n: (b, n, 0)),
            pl.BlockSpec((None, None, N_HEADS * BLOCK, KV_RANK), lambda b, n: (b, n, 0, 0)),
            pl.BlockSpec((None, None, IDX_HEADS * BLOCK, IDX_DIM), lambda b, n: (b, n, 0, 0)),
            pl.BlockSpec((None, IDX_HEADS, BLOCK), lambda b, n: (b, 0, n)),
            pl.BlockSpec((None, S, IDX_DIM), lambda b, n: (b, 0, 0)),
            pl.BlockSpec((None, S, KV_RANK), lambda b, n: (b, 0, 0)),
            pl.BlockSpec((None, S // DSA_KC, raug, DSA_KC), lambda b, n: (b, 0, 0, 0)),
            _resident((N_HEADS, 3 * DSA_KC, BLOCK)),
            _resident((DSA_KC, DSA_KC)),
            _resident((N_HEADS, HEAD_DIM, KV_RANK)),
            _resident((D, qdim)),
            _resident((1, D)),
        ],
        out_specs=pl.BlockSpec((None, BLOCK, D), lambda b, n: (b, n, 0)),
        scratch_shapes=[
            pltpu.VMEM((S, BLOCK), jnp.int32),
            pltpu.VMEM((S, BLOCK), jnp.int16),
            pltpu.VMEM((S, BLOCK), jnp.int16),
            pltpu.VMEM((N_HEADS // 2, raug, 2 * BLOCK), F32),
            pltpu.VMEM((N_HEADS, 1, BLOCK), F32),
            pltpu.VMEM((qdim, BLOCK), BF16),
            pltpu.VMEM((DSA_KC, BLOCK), F32),
            pltpu.VMEM((2, N_HEADS // 2, DSA_KC, 2 * BLOCK), BF16),
            pltpu.VMEM((2, N_HEADS // 2, 1, 2 * BLOCK), F32),
            pltpu.VMEM((2, DSA_KC, IDX_HEADS * BLOCK), F32),
        ],
        compiler_params=pltpu.CompilerParams(
            dimension_semantics=("parallel", "parallel"),
            vmem_limit_bytes=V7X_VMEM_LIMIT_BYTES),
        name="dsa_attn",
    )
    return attn(x, qabs, qidx, widxt, kidx, ckv, ckvt, bias_t, tri, wuv, wout_t, post_g.reshape(1, D))


def kernel(x, rel_bias, l0_norm_pre_mix, l0_norm_post_mix, l0_attn_w_in, l0_attn_sinks, l0_attn_w_out, l0_norm_pre_ffn, l0_norm_post_ffn, l0_ffn_w_up, l0_ffn_conv_w, l0_ffn_conv_b, l0_ffn_w_down, l1_norm_pre_mix, l1_norm_post_mix, l1_gmlp_w_in, l1_gmlp_ln_g, l1_gmlp_ln_b, l1_gmlp_w_s, l1_gmlp_b_s, l1_gmlp_w_out, l1_norm_pre_ffn, l1_norm_post_ffn, l1_ffn_w_up, l1_ffn_conv_w, l1_ffn_conv_b, l1_ffn_w_down, l2_norm_pre_mix, l2_norm_post_mix, l2_dsa_w_in, l2_dsa_kv_norm, l2_dsa_kidx_norm, l2_dsa_w_uk, l2_dsa_w_uv, l2_dsa_w_out, l2_norm_pre_ffn, l2_norm_post_ffn, l2_ffn_w_up, l2_ffn_conv_w, l2_ffn_conv_b, l2_ffn_w_down, l3_norm_pre_mix, l3_norm_post_mix, l3_attn_w_in, l3_attn_sinks, l3_attn_w_out, l3_norm_pre_ffn, l3_norm_post_ffn, l3_ffn_w_up, l3_ffn_conv_w, l3_ffn_conv_b, l3_ffn_w_down):
    x = _swa(x, l0_norm_pre_mix, l0_norm_post_mix, l0_attn_w_in, l0_attn_sinks, l0_attn_w_out, rel_bias)
    x = _ffn(x, l0_norm_pre_ffn, l0_norm_post_ffn, l0_ffn_w_up, l0_ffn_conv_w, l0_ffn_conv_b, l0_ffn_w_down)
    x = _gmlp(x, l1_norm_pre_mix, l1_norm_post_mix, l1_gmlp_w_in, l1_gmlp_ln_g, l1_gmlp_ln_b, l1_gmlp_w_s, l1_gmlp_b_s, l1_gmlp_w_out)
    x = _ffn(x, l1_norm_pre_ffn, l1_norm_post_ffn, l1_ffn_w_up, l1_ffn_conv_w, l1_ffn_conv_b, l1_ffn_w_down)
    x = _dsa(x, l2_norm_pre_mix, l2_norm_post_mix, l2_dsa_w_in, l2_dsa_kv_norm, l2_dsa_kidx_norm, l2_dsa_w_uk, l2_dsa_w_uv, l2_dsa_w_out, rel_bias)
    x = _ffn(x, l2_norm_pre_ffn, l2_norm_post_ffn, l2_ffn_w_up, l2_ffn_conv_w, l2_ffn_conv_b, l2_ffn_w_down)
    x = _swa(x, l3_norm_pre_mix, l3_norm_post_mix, l3_attn_w_in, l3_attn_sinks, l3_attn_w_out, rel_bias)
    x = _ffn(x, l3_norm_pre_ffn, l3_norm_post_ffn, l3_ffn_w_up, l3_ffn_conv_w, l3_ffn_conv_b, l3_ffn_w_down)
    return x
```

```python
import functools
import math

import jax
import jax.numpy as jnp
from jax import lax
from jax.experimental import pallas as pl
from jax.experimental.pallas import tpu as pltpu

F32 = jnp.float32
BF16 = jnp.bfloat16

EPS = 1e-6
N_HEADS = 16
HEAD_DIM = 64
N_KV_HEADS = 4
GQA_GROUP = N_HEADS // N_KV_HEADS
WINDOW = 128
BLOCK = 128
GMLP_CHUNK = 128
GMLP_GROUPS = 8
KV_RANK = 128
IDX_HEADS = 8
IDX_DIM = 64
TOPK_MAX = 256
REL_BUCKETS = 32
REL_MAX_DIST = 128
CONV_WIDTH = 3

V7X_LANES = 128
V7X_BF16_SUBLANES = 16
V7X_VMEM_LIMIT_BYTES = 56 * 1024 * 1024

LOG2E = math.log2(math.e)


def _rms(xf, g):
    ms = jnp.mean(xf * xf, axis=-1, keepdims=True)
    return xf * lax.rsqrt(ms + EPS) * g


def _resident(shape):
    nd = len(shape)
    return pl.BlockSpec(shape, lambda *_: (0,) * nd, pipeline_mode=pl.Buffered(1))


FFN_TM = 512
FFN_TN = 256
FFN_HALO = V7X_BF16_SUBLANES


def _ffn_body(x_ref, xh_ref, pre_ref, post_ref, wg_ref, wv_ref, cw_ref, cb_ref, wd_ref,
              o_ref, xn_sc, h_sc, a_sc, acc_sc, *, tm, nch, halo):
    i = pl.program_id(1)
    pre = pre_ref[...]
    xn_sc[halo:, :] = _rms(x_ref[...], pre).astype(BF16)
    xh = jnp.where(i > 0, _rms(xh_ref[...], pre), 0.0)
    xn_sc[:halo, :] = xh.astype(BF16)
    acc_sc[...] = jnp.zeros_like(acc_sc)

    def up(j, slot):
        xn = xn_sc[...]
        h_sc[slot, 0] = jnp.dot(xn, wg_ref[j], preferred_element_type=F32)
        h_sc[slot, 1] = jnp.dot(xn, wv_ref[j], preferred_element_type=F32)

    def conv(slot, part, w, b):
        h0 = h_sc[slot, part, halo:, :]
        h1 = h_sc[slot, part, halo - 1:halo - 1 + tm, :]
        h2 = h_sc[slot, part, halo - 2:halo - 2 + tm, :]
        return h0 * w[2:3, :] + h1 * w[1:2, :] + h2 * w[0:1, :] + b

    def act(j, slot):
        g = conv(slot, 0, cw_ref[j], cb_ref[j])
        v = conv(slot, 1, cw_ref[nch + j], cb_ref[nch + j])
        a_sc[slot] = (g * (1.0 / (1.0 + jnp.exp(-g))) * v).astype(BF16)

    def down(j, slot):
        acc_sc[...] += jnp.dot(a_sc[slot], wd_ref[j], preferred_element_type=F32)

    for j in range(nch + 2):
        if j >= 2:
            down(j - 2, j % 2)
        if 1 <= j <= nch:
            act(j - 1, (j - 1) % 2)
        if j < nch:
            up(j, j % 2)
    o_ref[...] = x_ref[...] + _rms(acc_sc[...], post_ref[...])


def _ffn(x, pre_g, post_g, w_up, conv_w, conv_b, w_down):
    B, S, D = x.shape
    dff = w_down.shape[0]
    tm, tn, halo = min(FFN_TM, S), FFN_TN, FFN_HALO
    nch = dff // tn
    assert dff % tn == 0 and S % tm == 0 and tm % halo == 0
    wup = w_up.astype(BF16).reshape(D, 2, nch, tn)
    wg = jnp.transpose(wup[:, 0], (1, 0, 2))
    wv = jnp.transpose(wup[:, 1], (1, 0, 2))
    wd = w_down.astype(BF16).reshape(nch, tn, D)
    cw = jnp.transpose(conv_w.reshape(CONV_WIDTH, 2 * nch, tn), (1, 0, 2))
    cb = conv_b.reshape(2 * nch, 1, tn)
    body = functools.partial(_ffn_body, tm=tm, nch=nch, halo=halo)
    return pl.pallas_call(
        body,
        out_shape=jax.ShapeDtypeStruct((B, S, D), F32),
        grid=(B, S // tm),
        in_specs=[
            pl.BlockSpec((None, tm, D), lambda b, i: (b, i, 0)),
            pl.BlockSpec((None, halo, D), lambda b, i: (b, jnp.maximum(i * (tm // halo) - 1, 0), 0)),
            _resident((1, D)), _resident((1, D)),
            _resident((nch, D, tn)), _resident((nch, D, tn)),
            _resident((2 * nch, CONV_WIDTH, tn)), _resident((2 * nch, 1, tn)),
            _resident((nch, tn, D)),
        ],
        out_specs=pl.BlockSpec((None, tm, D), lambda b, i: (b, i, 0)),
        scratch_shapes=[
            pltpu.VMEM((halo + tm, D), BF16),
            pltpu.VMEM((2, 2, halo + tm, tn), F32),
            pltpu.VMEM((2, tm, tn), BF16),
            pltpu.VMEM((tm, D), F32),
        ],
        compiler_params=pltpu.CompilerParams(
            dimension_semantics=("parallel", "parallel"),
            vmem_limit_bytes=V7X_VMEM_LIMIT_BYTES,
            ),
        name="ffn",
    )(x, x, pre_g.reshape(1, D), post_g.reshape(1, D), wg, wv, cw, cb, wd)


def _rel_bucket(dist):
    max_exact = REL_BUCKETS // 2
    d = jnp.maximum(dist, 0)
    df = jnp.maximum(d, 1).astype(F32)
    large = max_exact + (jnp.log(df / max_exact) / math.log(REL_MAX_DIST / max_exact)
                         * (REL_BUCKETS - max_exact)).astype(jnp.int32)
    large = jnp.minimum(large, REL_BUCKETS - 1)
    return jnp.where(d < max_exact, d, large)


def _bias_lookup(rel_bias, dist):
    onehot = _rel_bucket(dist)[None, ..., None] == jnp.arange(REL_BUCKETS)
    table = jnp.transpose(rel_bias.astype(F32)).reshape((rel_bias.shape[1],) + (1,) * dist.ndim + (REL_BUCKETS,))
    return jnp.sum(jnp.where(onehot, table, 0.0), axis=-1)


def _bias_toeplitz(rel_bias, base, rows, cols):
    n = rows + cols
    vals = _bias_lookup(rel_bias, jnp.arange(base - rows + 1, base + cols + 1))
    h = vals.shape[0]
    skew = jnp.broadcast_to(vals[:, None, :], (h, rows, n)).reshape(h, rows * n)[:, :rows * (n - 1)]
    return skew.reshape(h, rows, n - 1)[:, :, rows - 1:rows - 1 + cols]


SWA_TQ = 1024


def _swa_body(x_ref, pre_ref, post_ref, win_ref, sink_ref, bias_ref, wout_ref, o_ref,
              kv_sc, o_sc, s_sc, p_sc, *, tq):
    i = pl.program_id(1)
    qdim = N_HEADS * HEAD_DIM
    kdim = N_KV_HEADS * HEAD_DIM

    @pl.when(i == 0)
    def _():
        kv_sc[0] = jnp.zeros(kv_sc.shape[1:], BF16)

    x = x_ref[...]
    xn = _rms(x, pre_ref[...]).astype(BF16)
    qkv = jnp.dot(xn, win_ref[...], preferred_element_type=F32)
    q = (qkv[:, :qdim] * (HEAD_DIM ** -0.5 * LOG2E)).astype(BF16)
    kv = qkv[:, qdim:].astype(BF16)
    col = lax.broadcasted_iota(jnp.int32, (BLOCK, 2 * BLOCK), 1)
    first_masked = jnp.where(i == 0, BLOCK, 0)
    for r in range(tq // BLOCK):
        rows = slice(r * BLOCK, (r + 1) * BLOCK)
        kv_prev = kv_sc[i % 2] if r == 0 else kv[(r - 1) * BLOCK:r * BLOCK]
        kvw = jnp.concatenate([kv_prev, kv[rows]], axis=0)
        for h in range(N_HEADS):
            kvh = h // GQA_GROUP
            s = lax.dot_general(q[rows, h * HEAD_DIM:(h + 1) * HEAD_DIM],
                                kvw[:, kvh * HEAD_DIM:(kvh + 1) * HEAD_DIM],
                                (((1,), (1,)), ((), ())), preferred_element_type=F32) + bias_ref[h]
            if r == 0:
                s = jnp.where(col < first_masked, -jnp.inf, s)
            s_sc[h] = s
        for h in range(N_HEADS):
            s = s_sc[h]
            sink = sink_ref[h]
            m = jnp.maximum(jnp.max(s, axis=-1, keepdims=True), sink)
            p = jnp.exp2(s - m)
            den = jnp.sum(p, axis=-1, keepdims=True) + jnp.exp2(sink - m)
            p_sc[h] = (p * (1.0 / den)).astype(BF16)
        for h in range(N_HEADS):
            kvh = h // GQA_GROUP
            oh = jnp.dot(p_sc[h], kvw[:, kdim + kvh * HEAD_DIM:kdim + (kvh + 1) * HEAD_DIM],
                         preferred_element_type=F32)
            o_sc[rows, h * HEAD_DIM:(h + 1) * HEAD_DIM] = oh.astype(BF16)
    kv_sc[(i + 1) % 2] = kv[tq - BLOCK:]
    mo = jnp.dot(o_sc[...], wout_ref[...], preferred_element_type=F32)
    o_ref[...] = x + _rms(mo, post_ref[...])


def _swa(x, pre_g, post_g, w_in, sinks, w_out, rel_bias):
    B, S, D = x.shape
    tq = min(SWA_TQ, S)
    qdim, kdim = N_HEADS * HEAD_DIM, N_KV_HEADS * HEAD_DIM
    assert S % tq == 0 and tq % BLOCK == 0
    dist = (jnp.arange(BLOCK)[:, None] + BLOCK) - jnp.arange(2 * BLOCK)[None, :]
    band = (dist >= 0) & (dist < WINDOW)
    bias = jnp.transpose(_bias_toeplitz(rel_bias, BLOCK, 2 * BLOCK, BLOCK), (0, 2, 1))
    bias = jnp.where(band[None], bias * LOG2E, -jnp.inf)
    body = functools.partial(_swa_body, tq=tq)
    return pl.pallas_call(
        body,
        out_shape=jax.ShapeDtypeStruct((B, S, D), F32),
        grid=(B, S // tq),
        in_specs=[
            pl.BlockSpec((None, tq, D), lambda b, i: (b, i, 0)),
            _resident((1, D)), _resident((1, D)),
            _resident((D, qdim + 2 * kdim)),
            pl.BlockSpec(memory_space=pltpu.SMEM),
            _resident((N_HEADS, BLOCK, 2 * BLOCK)),
            _resident((qdim, D)),
        ],
        out_specs=pl.BlockSpec((None, tq, D), lambda b, i: (b, i, 0)),
        scratch_shapes=[
            pltpu.VMEM((2, BLOCK, 2 * kdim), BF16),
            pltpu.VMEM((tq, qdim), BF16),
            pltpu.VMEM((N_HEADS, BLOCK, 2 * BLOCK), F32),
            pltpu.VMEM((N_HEADS, BLOCK, 2 * BLOCK), BF16),
        ],
        compiler_params=pltpu.CompilerParams(
            dimension_semantics=("arbitrary", "arbitrary"),
            vmem_limit_bytes=V7X_VMEM_LIMIT_BYTES),
        name="swa",
    )(x, pre_g.reshape(1, D), post_g.reshape(1, D), w_in.astype(BF16), sinks.astype(F32) * LOG2E,
      bias, w_out.astype(BF16))


GMLP_TQ = 256


def _gmlp_body(x_ref, pre_ref, post_ref, win_ref, lng_ref, lnb_ref, ws_ref, bs_ref, wout_ref,
               o_ref, gated_sc, *, tq):
    x = x_ref[...]
    width = wout_ref.shape[0]
    gdim = width // GMLP_GROUPS
    xn = _rms(x, pre_ref[...]).astype(BF16)
    h = jax.nn.gelu(jnp.dot(xn, win_ref[...], preferred_element_type=F32))
    u, v = h[:, :width], h[:, width:]
    mu = jnp.mean(v, axis=-1, keepdims=True)
    var = jnp.mean(jnp.square(v - mu), axis=-1, keepdims=True)
    vn = ((v - mu) * lax.rsqrt(var + EPS) * lng_ref[...] + lnb_ref[...]).astype(BF16)
    for c in range(tq // GMLP_CHUNK):
        rows = slice(c * GMLP_CHUNK, (c + 1) * GMLP_CHUNK)
        for g in range(GMLP_GROUPS):
            cols = slice(g * gdim, (g + 1) * gdim)
            mixed = jnp.dot(ws_ref[g], vn[rows, cols], preferred_element_type=F32) + bs_ref[g]
            gated_sc[rows, cols] = (u[rows, cols] * mixed).astype(BF16)
    mo = jnp.dot(gated_sc[...], wout_ref[...], preferred_element_type=F32)
    o_ref[...] = x + _rms(mo, post_ref[...])


def _gmlp(x, pre_g, post_g, w_in, ln_g, ln_b, w_s, b_s, w_out):
    B, S, D = x.shape
    width = w_out.shape[0]
    gdim = width // GMLP_GROUPS
    tq = min(GMLP_TQ, S)
    assert S % tq == 0 and tq % GMLP_CHUNK == 0 and gdim == V7X_LANES
    causal = jnp.tril(jnp.ones((GMLP_CHUNK, GMLP_CHUNK), dtype=bool))
    ws = jnp.where(causal, w_s, jnp.zeros_like(w_s)).astype(BF16)
    bs = jnp.broadcast_to(b_s[:, :, None], (GMLP_GROUPS, GMLP_CHUNK, gdim)).astype(F32)
    body = functools.partial(_gmlp_body, tq=tq)
    return pl.pallas_call(
        body,
        out_shape=jax.ShapeDtypeStruct((B, S, D), F32),
        grid=(B, S // tq),
        in_specs=[
            pl.BlockSpec((None, tq, D), lambda b, i: (b, i, 0)),
            _resident((1, D)), _resident((1, D)),
            _resident((D, 2 * width)),
            _resident((1, width)), _resident((1, width)),
            _resident((GMLP_GROUPS, GMLP_CHUNK, GMLP_CHUNK)),
            _resident((GMLP_GROUPS, GMLP_CHUNK, gdim)),
            _resident((width, D)),
        ],
        out_specs=pl.BlockSpec((None, tq, D), lambda b, i: (b, i, 0)),
        scratch_shapes=[pltpu.VMEM((tq, width), BF16)],
        compiler_params=pltpu.CompilerParams(
            dimension_semantics=("parallel", "parallel"),
            vmem_limit_bytes=V7X_VMEM_LIMIT_BYTES),
        name="gmlp",
    )(x, pre_g.reshape(1, D), post_g.reshape(1, D), w_in.astype(BF16),
      ln_g.reshape(1, width), ln_b.reshape(1, width), ws, bs, w_out.astype(BF16))


DSA_TQ = 512
DSA_KC = 256
DSA_SMALL = 256
DSA_AUG = 16
INT_MIN = -2 ** 31
NEG_BIG = -1e30
COUNT_LANES = 8


def _dsa_proj_body(x_ref, pre_ref, wq_ref, wqi_ref, wsm_ref, kvg_ref, kig_ref, wuk_ref,
                   qabs_ref, ckv_ref, ckvt_ref, qidx_ref, kidx_ref, widxt_ref, *, tq):
    xn = _rms(x_ref[...], pre_ref[...]).astype(BF16)
    q = jnp.dot(xn, wq_ref[...], preferred_element_type=F32).astype(BF16)
    for h in range(N_HEADS):
        qa = jnp.dot(q[:, h * HEAD_DIM:(h + 1) * HEAD_DIM], wuk_ref[h],
                     preferred_element_type=F32) * (HEAD_DIM ** -0.5 * LOG2E)
        qa = qa.astype(BF16)
        for blk in range(tq // BLOCK):
            qabs_ref[blk, h * BLOCK:(h + 1) * BLOCK, :] = qa[blk * BLOCK:(blk + 1) * BLOCK]
    qi = jnp.dot(xn, wqi_ref[...], preferred_element_type=F32).astype(BF16)
    for h in range(IDX_HEADS):
        for blk in range(tq // BLOCK):
            qidx_ref[blk, h * BLOCK:(h + 1) * BLOCK, :] = qi[blk * BLOCK:(blk + 1) * BLOCK,
                                                              h * IDX_DIM:(h + 1) * IDX_DIM]
    sm = jnp.dot(xn, wsm_ref[...], preferred_element_type=F32)
    ckv = _rms(sm[:, :KV_RANK], kvg_ref[...])
    ckv_ref[...] = ckv.astype(BF16)
    ones_row = jnp.where(lax.broadcasted_iota(jnp.int32, (DSA_AUG, DSA_KC), 0) == 0,
                         jnp.float32(1.0), jnp.float32(0.0)).astype(BF16)
    for cc in range(tq // DSA_KC):
        ckvt_ref[cc, :KV_RANK, :] = jnp.transpose(ckv[cc * DSA_KC:(cc + 1) * DSA_KC]).astype(BF16)
        ckvt_ref[cc, KV_RANK:, :] = ones_row
    kidx_ref[...] = _rms(sm[:, KV_RANK:KV_RANK + IDX_DIM], kig_ref[...]).astype(BF16)
    tail = jnp.transpose(sm[:, KV_RANK:2 * KV_RANK])
    widxt_ref[...] = tail[IDX_DIM:IDX_DIM + IDX_HEADS, :] * (IDX_HEADS ** -0.5 * IDX_DIM ** -0.5)


def _dsa_attn_body(x_ref, qabs_ref, qidx_ref, widxt_ref, kidx_ref, ckv_ref, ckvt_ref, bias_ref,
                   tri_ref, wuv_ref, wout_ref, post_ref, o_ref,
                   keys_sc, hi_sc, lo_sc, acc_sc, m_sc, o_sc, am_sc, p_sc, alpha_sc, xt_sc, *, top_k):
    n = pl.program_id(1)
    kc = DSA_KC
    nch = (n + 2) // 2
    lane = lax.broadcasted_iota(jnp.int32, (kc, BLOCK), 1)
    row = lax.broadcasted_iota(jnp.int32, (kc, BLOCK), 0)
    qpos = n * BLOCK + lane
    nt = (((1,), (1,)), ((), ()))

    def scores(c, slot):
        xt_sc[slot] = lax.dot_general(kidx_ref[pl.ds(pl.multiple_of(c * kc, kc), kc), :], qidx_ref[...], nt,
                                      preferred_element_type=F32)

    def to_keys(c, slot):
        base = pl.multiple_of(c * kc, kc)
        acc = jnp.zeros((kc, BLOCK), F32)
        for h in range(IDX_HEADS):
            acc = acc + widxt_ref[h:h + 1, :] * jnp.maximum(xt_sc[slot, :, h * BLOCK:(h + 1) * BLOCK], 0.0)
        bits = pltpu.bitcast(acc, jnp.int32)
        key = bits ^ ((bits >> 31) & 0x7FFFFFFF)
        key = jnp.where(c * kc + row <= qpos, key, INT_MIN)
        keys_sc[pl.ds(base, kc), :] = key
        hi_sc[pl.ds(base, kc), :] = (key >> 16).astype(jnp.int16)

    def p1_pair(k, carry):
        c = 2 * k + 1
        to_keys(c - 1, 0)
        scores(c, 1)
        to_keys(c, 1)
        scores(c + 1, 0)
        return carry

    scores(0, 0)
    lax.fori_loop(0, (nch - 1) // 2, p1_pair, 0)

    @pl.when(nch % 2 == 0)
    def _():
        to_keys(nch - 2, 0)
        scores(nch - 1, 1)
        to_keys(nch - 1, 1)

    @pl.when(nch % 2 == 1)
    def _():
        to_keys(nch - 1, 0)

    def chunks(body, init):
        return lax.fori_loop(0, nch, lambda c, v: body(pl.multiple_of(c * kc, kc), v), init)

    def count_ge16(src_ref, cand):
        c16 = cand.astype(jnp.int16)

        def body(base, cnts):
            v = src_ref[pl.ds(base, kc), :]
            cnts = list(cnts)
            for j in range(kc // 16):
                a = j % COUNT_LANES
                cnts[a] = jnp.where(v[j * 16:(j + 1) * 16, :] >= c16, cnts[a] + jnp.int16(1), cnts[a])
            return tuple(cnts)

        cnts = chunks(body, (jnp.zeros((16, BLOCK), jnp.int16),) * COUNT_LANES)
        total = functools.reduce(lambda u, v: u + v, [c.astype(jnp.int32) for c in cnts])
        return jnp.sum(total, axis=0, keepdims=True, dtype=jnp.int32)

    def kth_largest16(src_ref, kth):
        def bit(it, u):
            cu = u | jnp.left_shift(jnp.int32(1), 15 - it)
            return jnp.where(count_ge16(src_ref, cu - 2 ** 15) >= kth, cu, u)
        return lax.fori_loop(0, 16, bit, jnp.zeros((1, BLOCK), jnp.int32)) - 2 ** 15

    def count_gt32(cand):
        def body(base, cnts):
            k = keys_sc[pl.ds(base, kc), :]
            cnts = list(cnts)
            for j in range(kc // 8):
                a = j % COUNT_LANES
                cnts[a] = jnp.where(k[j * 8:(j + 1) * 8, :] > cand, cnts[a] + 1, cnts[a])
            return tuple(cnts)
        cnts = chunks(body, (jnp.zeros((8, BLOCK), jnp.int32),) * COUNT_LANES)
        return jnp.sum(functools.reduce(lambda u, v: u + v, cnts), axis=0, keepdims=True, dtype=jnp.int32)

    hi = kth_largest16(hi_sc, top_k)
    above = jnp.where(hi >= 2 ** 15 - 1, 0, count_ge16(hi_sc, jnp.minimum(hi + 1, 2 ** 15 - 1)))

    def low_halves(base, carry):
        k = keys_sc[pl.ds(base, kc), :]
        lo = jnp.where((k >> 16) == hi, (k & 0xFFFF) - 2 ** 15, -2 ** 15)
        lo_sc[pl.ds(base, kc), :] = lo.astype(jnp.int16)
        return carry

    chunks(low_halves, 0)
    lo = kth_largest16(lo_sc, top_k - above)
    thr = hi * 2 ** 16 + (lo + 2 ** 15)
    need = (top_k - count_gt32(thr)).astype(F32)

    acc_sc[...] = jnp.zeros_like(acc_sc)
    m_sc[...] = jnp.full_like(m_sc, NEG_BIG)

    def probs(c, slot, eq_seen, far=False):
        base = pl.multiple_of(c * kc, kc)
        key = keys_sc[pl.ds(base, kc), :]
        eq = key == thr
        eqf = jnp.where(eq, jnp.float32(1.0), jnp.float32(0.0))
        rank = jnp.dot(tri_ref[...], eqf.astype(BF16), preferred_element_type=F32) + eq_seen
        sel = jnp.logical_or(key > thr, jnp.logical_and(eq, rank < need))
        sel = jnp.logical_and(sel, key != INT_MIN)
        am_sc[...] = jnp.where(sel, jnp.float32(0.0), jnp.float32(-jnp.inf))
        ckc = ckv_ref[pl.ds(base, kc), :]
        off = pl.multiple_of(2 * kc - jnp.minimum(n * BLOCK - c * kc, 2 * kc), BLOCK)
        for hp in range(N_HEADS // 2):
            lt2 = lax.dot_general(ckc, qabs_ref[hp * 2 * BLOCK:(hp + 1) * 2 * BLOCK, :], nt,
                                  preferred_element_type=F32)
            for j in range(2):
                h = 2 * hp + j
                ls = slice(j * BLOCK, (j + 1) * BLOCK)
                if far:
                    lt = lt2[:, ls] + am_sc[...]
                else:
                    lt = lt2[:, ls] + (am_sc[...] + bias_ref[h, pl.ds(off, kc), :])
                m_old = m_sc[h]
                m_new = jnp.maximum(m_old, jnp.max(lt, axis=0, keepdims=True))
                m_sc[h] = m_new
                alpha_sc[slot, hp, :, ls] = jnp.exp2(m_old - m_new)
                p_sc[slot, hp, :, ls] = jnp.exp2(lt - m_new).astype(BF16)
        return eq_seen + jnp.sum(eqf, axis=0, keepdims=True)

    def accumulate(c, slot):
        ckt = ckvt_ref[c]
        for hp in range(N_HEADS // 2):
            acc_sc[hp] = alpha_sc[slot, hp] * acc_sc[hp] + jnp.dot(
                ckt, p_sc[slot, hp], preferred_element_type=F32)

    def step01(c, eq_seen, far=False):
        accumulate(c - 1, 0)
        return probs(c, 1, eq_seen, far)

    def step10(c, eq_seen, far=False):
        accumulate(c - 1, 1)
        return probs(c, 0, eq_seen, far)

    def pair(far, k, eq_seen):
        return step10(2 * k + 2, step01(2 * k + 1, eq_seen, far), far)

    nfar = jnp.maximum((n - 2) // 2, 0)
    kfar = jnp.maximum((nfar - 1) // 2, 0)
    eq_seen = probs(0, 0, jnp.zeros((1, BLOCK), F32))
    eq_seen = lax.fori_loop(0, kfar, functools.partial(pair, True), eq_seen)
    eq_seen = lax.fori_loop(kfar, (nch - 1) // 2, functools.partial(pair, False), eq_seen)

    @pl.when(nch % 2 == 0)
    def _():
        step01(nch - 1, eq_seen)
        accumulate(nch - 1, 1)

    @pl.when(nch % 2 == 1)
    def _():
        accumulate(nch - 1, 0)

    for hp in range(N_HEADS // 2):
        a = acc_sc[hp]
        ot = (a[:KV_RANK] * (1.0 / a[KV_RANK:KV_RANK + 1])).astype(BF16)
        for j in range(2):
            h = 2 * hp + j
            o_sc[h * HEAD_DIM:(h + 1) * HEAD_DIM, :] = jnp.dot(
                wuv_ref[h], ot[:, j * BLOCK:(j + 1) * BLOCK],
                preferred_element_type=F32).astype(BF16)
    mt = jnp.dot(wout_ref[...], o_sc[...], preferred_element_type=F32)
    o_ref[...] = x_ref[...] + _rms(jnp.transpose(mt), post_ref[...])


def _dsa(x, pre_g, post_g, w_in, kv_g, ki_g, w_uk, w_uv, w_out, rel_bias):
    B, S, D = x.shape
    qdim = N_HEADS * HEAD_DIM
    qidim = IDX_HEADS * IDX_DIM
    tq = min(DSA_TQ, S)
    top_k = min(TOPK_MAX, S // 4)
    assert S % tq == 0 and tq % DSA_KC == 0 and KV_RANK == BLOCK and DSA_KC == 2 * BLOCK
    c0, c1, c2, c3 = qdim, qdim + KV_RANK, qdim + KV_RANK + qidim, qdim + KV_RANK + qidim + IDX_DIM
    wq = w_in[:, :c0].astype(BF16)
    wqi = w_in[:, c1:c2].astype(BF16)
    wsm = jnp.concatenate([w_in[:, c0:c1], w_in[:, c2:c3], w_in[:, c3:]], axis=1)
    wsm = jnp.pad(wsm, ((0, 0), (0, DSA_SMALL - wsm.shape[1]))).astype(BF16)
    wuk = jnp.transpose(w_uk, (1, 2, 0)).astype(BF16)
    wuv = jnp.transpose(w_uv, (1, 2, 0)).astype(BF16)
    wout_t = jnp.transpose(w_out).astype(BF16)
    nblk = S // BLOCK
    raug = KV_RANK + DSA_AUG

    proj = pl.pallas_call(
        functools.partial(_dsa_proj_body, tq=tq),
        out_shape=[
            jax.ShapeDtypeStruct((B, nblk, N_HEADS * BLOCK, KV_RANK), BF16),
            jax.ShapeDtypeStruct((B, S, KV_RANK), BF16),
            jax.ShapeDtypeStruct((B, S // DSA_KC, raug, DSA_KC), BF16),
            jax.ShapeDtypeStruct((B, nblk, IDX_HEADS * BLOCK, IDX_DIM), BF16),
            jax.ShapeDtypeStruct((B, S, IDX_DIM), BF16),
            jax.ShapeDtypeStruct((B, IDX_HEADS, S), F32),
        ],
        grid=(B, S // tq),
        in_specs=[
            pl.BlockSpec((None, tq, D), lambda b, i: (b, i, 0)),
            _resident((1, D)),
            _resident((D, qdim)), _resident((D, qidim)), _resident((D, DSA_SMALL)),
            _resident((1, KV_RANK)), _resident((1, IDX_DIM)),
            _resident((N_HEADS, HEAD_DIM, KV_RANK)),
        ],
        out_specs=[
            pl.BlockSpec((None, tq // BLOCK, N_HEADS * BLOCK, KV_RANK), lambda b, i: (b, i, 0, 0)),
            pl.BlockSpec((None, tq, KV_RANK), lambda b, i: (b, i, 0)),
            pl.BlockSpec((None, tq // DSA_KC, raug, DSA_KC), lambda b, i: (b, i, 0, 0)),
            pl.BlockSpec((None, tq // BLOCK, IDX_HEADS * BLOCK, IDX_DIM), lambda b, i: (b, i, 0, 0)),
            pl.BlockSpec((None, tq, IDX_DIM), lambda b, i: (b, i, 0)),
            pl.BlockSpec((None, IDX_HEADS, tq), lambda b, i: (b, 0, i)),
        ],
        compiler_params=pltpu.CompilerParams(
            dimension_semantics=("parallel", "parallel"),
            vmem_limit_bytes=V7X_VMEM_LIMIT_BYTES),
        name="dsa_proj",
    )
    qabs, ckv, ckvt, qidx, kidx, widxt = proj(
        x, pre_g.reshape(1, D), wq, wqi, wsm, kv_g.reshape(1, KV_RANK), ki_g.reshape(1, IDX_DIM), wuk)

    far = rel_bias[REL_BUCKETS - 1].astype(F32)
    bias_t = (_bias_toeplitz(rel_bias, 2 * DSA_KC, 3 * DSA_KC, BLOCK) - far[:, None, None]) * LOG2E
    tri = jnp.tril(jnp.ones((DSA_KC, DSA_KC), F32), -1).astype(BF16)

    attn = pl.pallas_call(
        functools.partial(_dsa_attn_body, top_k=top_k),
        out_shape=jax.ShapeDtypeStruct((B, S, D), F32),
        grid=(B, nblk),
        in_specs=[
            pl.BlockSpec((None, BLOCK, D), lambda b, n: (b, n, 0)),
            pl.BlockSpec((None, None, N_HEADS * BLOCK, KV_RANK), lambda b, n: (b, n, 0, 0)),
            pl.BlockSpec((None, None, IDX_HEADS * BLOCK, IDX_DIM), lambda b, n: (b, n, 0, 0)),
            pl.BlockSpec((None, IDX_HEADS, BLOCK), lambda b, n: (b, 0, n)),
            pl.BlockSpec((None, S, IDX_DIM), lambda b, n: (b, 0, 0)),
            pl.BlockSpec((None, S, KV_RANK), lambda b, n: (b, 0, 0)),
            pl.BlockSpec((None, S // DSA_KC, raug, DSA_KC), lambda b, n: (b, 0, 0, 0)),
            _resident((N_HEADS, 3 * DSA_KC, BLOCK)),
            _resident((DSA_KC, DSA_KC)),
            _resident((N_HEADS, HEAD_DIM, KV_RANK)),
            _resident((D, qdim)),
            _resident((1, D)),
        ],
        out_specs=pl.BlockSpec((None, BLOCK, D), lambda b, n: (b, n, 0)),
        scratch_shapes=[
            pltpu.VMEM((S, BLOCK), jnp.int32),
            pltpu.VMEM((S, BLOCK), jnp.int16),
            pltpu.VMEM((S, BLOCK), jnp.int16),
            pltpu.VMEM((N_HEADS // 2, raug, 2 * BLOCK), F32),
            pltpu.VMEM((N_HEADS, 1, BLOCK), F32),
            pltpu.VMEM((qdim, BLOCK), BF16),
            pltpu.VMEM((DSA_KC, BLOCK), F32),
            pltpu.VMEM((2, N_HEADS // 2, DSA_KC, 2 * BLOCK), BF16),
            pltpu.VMEM((2, N_HEADS // 2, 1, 2 * BLOCK), F32),
            pltpu.VMEM((2, DSA_KC, IDX_HEADS * BLOCK), F32),
        ],
        compiler_params=pltpu.CompilerParams(
            dimension_semantics=("parallel", "parallel"),
            vmem_limit_bytes=V7X_VMEM_LIMIT_BYTES),
        name="dsa_attn",
    )
    return attn(x, qabs, qidx, widxt, kidx, ckv, ckvt, bias_t, tri, wuv, wout_t, post_g.reshape(1, D))


def kernel(x, rel_bias, l0_norm_pre_mix, l0_norm_post_mix, l0_attn_w_in, l0_attn_sinks, l0_attn_w_out, l0_norm_pre_ffn, l0_norm_post_ffn, l0_ffn_w_up, l0_ffn_conv_w, l0_ffn_conv_b, l0_ffn_w_down, l1_norm_pre_mix, l1_norm_post_mix, l1_gmlp_w_in, l1_gmlp_ln_g, l1_gmlp_ln_b, l1_gmlp_w_s, l1_gmlp_b_s, l1_gmlp_w_out, l1_norm_pre_ffn, l1_norm_post_ffn, l1_ffn_w_up, l1_ffn_conv_w, l1_ffn_conv_b, l1_ffn_w_down, l2_norm_pre_mix, l2_norm_post_mix, l2_dsa_w_in, l2_dsa_kv_norm, l2_dsa_kidx_norm, l2_dsa_w_uk, l2_dsa_w_uv, l2_dsa_w_out, l2_norm_pre_ffn, l2_norm_post_ffn, l2_ffn_w_up, l2_ffn_conv_w, l2_ffn_conv_b, l2_ffn_w_down, l3_norm_pre_mix, l3_norm_post_mix, l3_attn_w_in, l3_attn_sinks, l3_attn_w_out, l3_norm_pre_ffn, l3_norm_post_ffn, l3_ffn_w_up, l3_ffn_conv_w, l3_ffn_conv_b, l3_ffn_w_down):
    x = _swa(x, l0_norm_pre_mix, l0_norm_post_mix, l0_attn_w_in, l0_attn_sinks, l0_attn_w_out, rel_bias)
    x = _ffn(x, l0_norm_pre_ffn, l0_norm_post_ffn, l0_ffn_w_up, l0_ffn_conv_w, l0_ffn_conv_b, l0_ffn_w_down)
    x = _gmlp(x, l1_norm_pre_mix, l1_norm_post_mix, l1_gmlp_w_in, l1_gmlp_ln_g, l1_gmlp_ln_b, l1_gmlp_w_s, l1_gmlp_b_s, l1_gmlp_w_out)
    x = _ffn(x, l1_norm_pre_ffn, l1_norm_post_ffn, l1_ffn_w_up, l1_ffn_conv_w, l1_ffn_conv_b, l1_ffn_w_down)
    x = _dsa(x, l2_norm_pre_mix, l2_norm_post_mix, l2_dsa_w_in, l2_dsa_kv_norm, l2_dsa_kidx_norm, l2_dsa_w_uk, l2_dsa_w_uv, l2_dsa_w_out, rel_bias)
    x = _ffn(x, l2_norm_pre_ffn, l2_norm_post_ffn, l2_ffn_w_up, l2_ffn_conv_w, l2_ffn_conv_b, l2_ffn_w_down)
    x = _swa(x, l3_norm_pre_mix, l3_norm_post_mix, l3_attn_w_in, l3_attn_sinks, l3_attn_w_out, rel_bias)
    x = _ffn(x, l3_norm_pre_ffn, l3_norm_post_ffn, l3_ffn_w_up, l3_ffn_conv_w, l3_ffn_conv_b, l3_ffn_w_down)
    return x
```

```python
import functools
import math

import jax
import jax.numpy as jnp
from jax import lax
from jax.experimental import pallas as pl
from jax.experimental.pallas import tpu as pltpu

F32 = jnp.float32
BF16 = jnp.bfloat16

EPS = 1e-6
N_HEADS = 16
HEAD_DIM = 64
N_KV_HEADS = 4
GQA_GROUP = N_HEADS // N_KV_HEADS
WINDOW = 128
BLOCK = 128
GMLP_CHUNK = 128
GMLP_GROUPS = 8
KV_RANK = 128
IDX_HEADS = 8
IDX_DIM = 64
TOPK_MAX = 256
REL_BUCKETS = 32
REL_MAX_DIST = 128
CONV_WIDTH = 3

V7X_LANES = 128
V7X_BF16_SUBLANES = 16
V7X_VMEM_LIMIT_BYTES = 56 * 1024 * 1024

LOG2E = math.log2(math.e)


def _rms(xf, g):
    ms = jnp.mean(xf * xf, axis=-1, keepdims=True)
    return xf * lax.rsqrt(ms + EPS) * g


def _resident(shape):
    nd = len(shape)
    return pl.BlockSpec(shape, lambda *_: (0,) * nd, pipeline_mode=pl.Buffered(1))


FFN_TM = 512
FFN_TN = 256
FFN_HALO = V7X_BF16_SUBLANES


def _ffn_body(x_ref, xh_ref, pre_ref, post_ref, wg_ref, wv_ref, cw_ref, cb_ref, wd_ref,
              o_ref, xn_sc, h_sc, a_sc, acc_sc, *, tm, nch, halo):
    i = pl.program_id(1)
    pre = pre_ref[...]
    xn_sc[halo:, :] = _rms(x_ref[...], pre).astype(BF16)
    xh = jnp.where(i > 0, _rms(xh_ref[...], pre), 0.0)
    xn_sc[:halo, :] = xh.astype(BF16)
    acc_sc[...] = jnp.zeros_like(acc_sc)

    def up(j, slot):
        xn = xn_sc[...]
        h_sc[slot, 0] = jnp.dot(xn, wg_ref[j], preferred_element_type=F32)
        h_sc[slot, 1] = jnp.dot(xn, wv_ref[j], preferred_element_type=F32)

    def conv(slot, part, w, b):
        h0 = h_sc[slot, part, halo:, :]
        h1 = h_sc[slot, part, halo - 1:halo - 1 + tm, :]
        h2 = h_sc[slot, part, halo - 2:halo - 2 + tm, :]
        return h0 * w[2:3, :] + h1 * w[1:2, :] + h2 * w[0:1, :] + b

    def act(j, slot):
        g = conv(slot, 0, cw_ref[j], cb_ref[j])
        v = conv(slot, 1, cw_ref[nch + j], cb_ref[nch + j])
        a_sc[slot] = (g * (1.0 / (1.0 + jnp.exp(-g))) * v).astype(BF16)

    def down(j, slot):
        acc_sc[...] += jnp.dot(a_sc[slot], wd_ref[j], preferred_element_type=F32)

    for j in range(nch + 2):
        if j >= 2:
            down(j - 2, j % 2)
        if 1 <= j <= nch:
            act(j - 1, (j - 1) % 2)
        if j < nch:
            up(j, j % 2)
    o_ref[...] = x_ref[...] + _rms(acc_sc[...], post_ref[...])


def _ffn(x, pre_g, post_g, w_up, conv_w, conv_b, w_down):
    B, S, D = x.shape
    dff = w_down.shape[0]
    tm, tn, halo = min(FFN_TM, S), FFN_TN, FFN_HALO
    nch = dff // tn
    assert dff % tn == 0 and S % tm == 0 and tm % halo == 0
    wup = w_up.astype(BF16).reshape(D, 2, nch, tn)
    wg = jnp.transpose(wup[:, 0], (1, 0, 2))
    wv = jnp.transpose(wup[:, 1], (1, 0, 2))
    wd = w_down.astype(BF16).reshape(nch, tn, D)
    cw = jnp.transpose(conv_w.reshape(CONV_WIDTH, 2 * nch, tn), (1, 0, 2))
    cb = conv_b.reshape(2 * nch, 1, tn)
    body = functools.partial(_ffn_body, tm=tm, nch=nch, halo=halo)
    return pl.pallas_call(
        body,
        out_shape=jax.ShapeDtypeStruct((B, S, D), F32),
        grid=(B, S // tm),
        in_specs=[
            pl.BlockSpec((None, tm, D), lambda b, i: (b, i, 0)),
            pl.BlockSpec((None, halo, D), lambda b, i: (b, jnp.maximum(i * (tm // halo) - 1, 0), 0)),
            _resident((1, D)), _resident((1, D)),
            _resident((nch, D, tn)), _resident((nch, D, tn)),
            _resident((2 * nch, CONV_WIDTH, tn)), _resident((2 * nch, 1, tn)),
            _resident((nch, tn, D)),
        ],
        out_specs=pl.BlockSpec((None, tm, D), lambda b, i: (b, i, 0)),
        scratch_shapes=[
            pltpu.VMEM((halo + tm, D), BF16),
            pltpu.VMEM((2, 2, halo + tm, tn), F32),
            pltpu.VMEM((2, tm, tn), BF16),
            pltpu.VMEM((tm, D), F32),
        ],
        compiler_params=pltpu.CompilerParams(
            dimension_semantics=("parallel", "parallel"),
            vmem_limit_bytes=V7X_VMEM_LIMIT_BYTES,
            ),
        name="ffn",
    )(x, x, pre_g.reshape(1, D), post_g.reshape(1, D), wg, wv, cw, cb, wd)


def _rel_bucket(dist):
    max_exact = REL_BUCKETS // 2
    d = jnp.maximum(dist, 0)
    df = jnp.maximum(d, 1).astype(F32)
    large = max_exact + (jnp.log(df / max_exact) / math.log(REL_MAX_DIST / max_exact)
                         * (REL_BUCKETS - max_exact)).astype(jnp.int32)
    large = jnp.minimum(large, REL_BUCKETS - 1)
    return jnp.where(d < max_exact, d, large)


def _bias_lookup(rel_bias, dist):
    onehot = _rel_bucket(dist)[None, ..., None] == jnp.arange(REL_BUCKETS)
    table = jnp.transpose(rel_bias.astype(F32)).reshape((rel_bias.shape[1],) + (1,) * dist.ndim + (REL_BUCKETS,))
    return jnp.sum(jnp.where(onehot, table, 0.0), axis=-1)


def _bias_toeplitz(rel_bias, base, rows, cols):
    n = rows + cols
    vals = _bias_lookup(rel_bias, jnp.arange(base - rows + 1, base + cols + 1))
    h = vals.shape[0]
    skew = jnp.broadcast_to(vals[:, None, :], (h, rows, n)).reshape(h, rows * n)[:, :rows * (n - 1)]
    return skew.reshape(h, rows, n - 1)[:, :, rows - 1:rows - 1 + cols]


SWA_TQ = 1024


def _swa_body(x_ref, pre_ref, post_ref, win_ref, sink_ref, bias_ref, wout_ref, o_ref,
              kv_sc, o_sc, s_sc, p_sc, *, tq):
    i = pl.program_id(1)
    qdim = N_HEADS * HEAD_DIM
    kdim = N_KV_HEADS * HEAD_DIM

    @pl.when(i == 0)
    def _():
        kv_sc[0] = jnp.zeros(kv_sc.shape[1:], BF16)

    x = x_ref[...]
    xn = _rms(x, pre_ref[...]).astype(BF16)
    qkv = jnp.dot(xn, win_ref[...], preferred_element_type=F32)
    q = (qkv[:, :qdim] * (HEAD_DIM ** -0.5 * LOG2E)).astype(BF16)
    kv = qkv[:, qdim:].astype(BF16)
    col = lax.broadcasted_iota(jnp.int32, (BLOCK, 2 * BLOCK), 1)
    first_masked = jnp.where(i == 0, BLOCK, 0)
    for r in range(tq // BLOCK):
        rows = slice(r * BLOCK, (r + 1) * BLOCK)
        kv_prev = kv_sc[i % 2] if r == 0 else kv[(r - 1) * BLOCK:r * BLOCK]
        kvw = jnp.concatenate([kv_prev, kv[rows]], axis=0)
        for h in range(N_HEADS):
            kvh = h // GQA_GROUP
            s = lax.dot_general(q[rows, h * HEAD_DIM:(h + 1) * HEAD_DIM],
                                kvw[:, kvh * HEAD_DIM:(kvh + 1) * HEAD_DIM],
                                (((1,), (1,)), ((), ())), preferred_element_type=F32) + bias_ref[h]
            if r == 0:
                s = jnp.where(col < first_masked, -jnp.inf, s)
            s_sc[h] = s
        for h in range(N_HEADS):
            s = s_sc[h]
            sink = sink_ref[h]
            m = jnp.maximum(jnp.max(s, axis=-1, keepdims=True), sink)
            p = jnp.exp2(s - m)
            den = jnp.sum(p, axis=-1, keepdims=True) + jnp.exp2(sink - m)
            p_sc[h] = (p * (1.0 / den)).astype(BF16)
        for h in range(N_HEADS):
            kvh = h // GQA_GROUP
            oh = jnp.dot(p_sc[h], kvw[:, kdim + kvh * HEAD_DIM:kdim + (kvh + 1) * HEAD_DIM],
                         preferred_element_type=F32)
            o_sc[rows, h * HEAD_DIM:(h + 1) * HEAD_DIM] = oh.astype(BF16)
    kv_sc[(i + 1) % 2] = kv[tq - BLOCK:]
    mo = jnp.dot(o_sc[...], wout_ref[...], preferred_element_type=F32)
    o_ref[...] = x + _rms(mo, post_ref[...])


def _swa(x, pre_g, post_g, w_in, sinks, w_out, rel_bias):
    B, S, D = x.shape
    tq = min(SWA_TQ, S)
    qdim, kdim = N_HEADS * HEAD_DIM, N_KV_HEADS * HEAD_DIM
    assert S % tq == 0 and tq % BLOCK == 0
    dist = (jnp.arange(BLOCK)[:, None] + BLOCK) - jnp.arange(2 * BLOCK)[None, :]
    band = (dist >= 0) & (dist < WINDOW)
    bias = jnp.transpose(_bias_toeplitz(rel_bias, BLOCK, 2 * BLOCK, BLOCK), (0, 2, 1))
    bias = jnp.where(band[None], bias * LOG2E, -jnp.inf)
    body = functools.partial(_swa_body, tq=tq)
    return pl.pallas_call(
        body,
        out_shape=jax.ShapeDtypeStruct((B, S, D), F32),
        grid=(B, S // tq),
        in_specs=[
            pl.BlockSpec((None, tq, D), lambda b, i: (b, i, 0)),
            _resident((1, D)), _resident((1, D)),
            _resident((D, qdim + 2 * kdim)),
            pl.BlockSpec(memory_space=pltpu.SMEM),
            _resident((N_HEADS, BLOCK, 2 * BLOCK)),
            _resident((qdim, D)),
        ],
        out_specs=pl.BlockSpec((None, tq, D), lambda b, i: (b, i, 0)),
        scratch_shapes=[
            pltpu.VMEM((2, BLOCK, 2 * kdim), BF16),
            pltpu.VMEM((tq, qdim), BF16),
            pltpu.VMEM((N_HEADS, BLOCK, 2 * BLOCK), F32),
            pltpu.VMEM((N_HEADS, BLOCK, 2 * BLOCK), BF16),
        ],
        compiler_params=pltpu.CompilerParams(
            dimension_semantics=("arbitrary", "arbitrary"),
            vmem_limit_bytes=V7X_VMEM_LIMIT_BYTES),
        name="swa",
    )(x, pre_g.reshape(1, D), post_g.reshape(1, D), w_in.astype(BF16), sinks.astype(F32) * LOG2E,
      bias, w_out.astype(BF16))


GMLP_TQ = 256


def _gmlp_body(x_ref, pre_ref, post_ref, win_ref, lng_ref, lnb_ref, ws_ref, bs_ref, wout_ref,
               o_ref, gated_sc, *, tq):
    x = x_ref[...]
    width = wout_ref.shape[0]
    gdim = width // GMLP_GROUPS
    xn = _rms(x, pre_ref[...]).astype(BF16)
    h = jax.nn.gelu(jnp.dot(xn, win_ref[...], preferred_element_type=F32))
    u, v = h[:, :width], h[:, width:]
    mu = jnp.mean(v, axis=-1, keepdims=True)
    var = jnp.mean(jnp.square(v - mu), axis=-1, keepdims=True)
    vn = ((v - mu) * lax.rsqrt(var + EPS) * lng_ref[...] + lnb_ref[...]).astype(BF16)
    for c in range(tq // GMLP_CHUNK):
        rows = slice(c * GMLP_CHUNK, (c + 1) * GMLP_CHUNK)
        for g in range(GMLP_GROUPS):
            cols = slice(g * gdim, (g + 1) * gdim)
            mixed = jnp.dot(ws_ref[g], vn[rows, cols], preferred_element_type=F32) + bs_ref[g]
            gated_sc[rows, cols] = (u[rows, cols] * mixed).astype(BF16)
    mo = jnp.dot(gated_sc[...], wout_ref[...], preferred_element_type=F32)
    o_ref[...] = x + _rms(mo, post_ref[...])


def _gmlp(x, pre_g, post_g, w_in, ln_g, ln_b, w_s, b_s, w_out):
    B, S, D = x.shape
    width = w_out.shape[0]
    gdim = width // GMLP_GROUPS
    tq = min(GMLP_TQ, S)
    assert S % tq == 0 and tq % GMLP_CHUNK == 0 and gdim == V7X_LANES
    causal = jnp.tril(jnp.ones((GMLP_CHUNK, GMLP_CHUNK), dtype=bool))
    ws = jnp.where(causal, w_s, jnp.zeros_like(w_s)).astype(BF16)
    bs = jnp.broadcast_to(b_s[:, :, None], (GMLP_GROUPS, GMLP_CHUNK, gdim)).astype(F32)
    body = functools.partial(_gmlp_body, tq=tq)
    return pl.pallas_call(
        body,
        out_shape=jax.ShapeDtypeStruct((B, S, D), F32),
        grid=(B, S // tq),
        in_specs=[
            pl.BlockSpec((None, tq, D), lambda b, i: (b, i, 0)),
            _resident((1, D)), _resident((1, D)),
            _resident((D, 2 * width)),
            _resident((1, width)), _resident((1, width)),
            _resident((GMLP_GROUPS, GMLP_CHUNK, GMLP_CHUNK)),
            _resident((GMLP_GROUPS, GMLP_CHUNK, gdim)),
            _resident((width, D)),
        ],
        out_specs=pl.BlockSpec((None, tq, D), lambda b, i: (b, i, 0)),
        scratch_shapes=[pltpu.VMEM((tq, width), BF16)],
        compiler_params=pltpu.CompilerParams(
            dimension_semantics=("parallel", "parallel"),
            vmem_limit_bytes=V7X_VMEM_LIMIT_BYTES),
        name="gmlp",
    )(x, pre_g.reshape(1, D), post_g.reshape(1, D), w_in.astype(BF16),
      ln_g.reshape(1, width), ln_b.reshape(1, width), ws, bs, w_out.astype(BF16))


DSA_TQ = 512
DSA_KC = 256
DSA_SMALL = 256
DSA_AUG = 16
INT_MIN = -2 ** 31
NEG_BIG = -1e30
COUNT_LANES = 4


def _dsa_proj_body(x_ref, pre_ref, wq_ref, wqi_ref, wsm_ref, kvg_ref, kig_ref, wuk_ref,
                   qabs_ref, ckv_ref, ckvt_ref, qidx_ref, kidx_ref, widxt_ref, *, tq):
    xn = _rms(x_ref[...], pre_ref[...]).astype(BF16)
    q = jnp.dot(xn, wq_ref[...], preferred_element_type=F32).astype(BF16)
    for h in range(N_HEADS):
        qa = jnp.dot(q[:, h * HEAD_DIM:(h + 1) * HEAD_DIM], wuk_ref[h],
                     preferred_element_type=F32) * (HEAD_DIM ** -0.5 * LOG2E)
        qa = qa.astype(BF16)
        for blk in range(tq // BLOCK):
            qabs_ref[blk, h * BLOCK:(h + 1) * BLOCK, :] = qa[blk * BLOCK:(blk + 1) * BLOCK]
    qi = jnp.dot(xn, wqi_ref[...], preferred_element_type=F32).astype(BF16)
    for h in range(IDX_HEADS):
        for blk in range(tq // BLOCK):
            qidx_ref[blk, h * BLOCK:(h + 1) * BLOCK, :] = qi[blk * BLOCK:(blk + 1) * BLOCK,
                                                              h * IDX_DIM:(h + 1) * IDX_DIM]
    sm = jnp.dot(xn, wsm_ref[...], preferred_element_type=F32)
    ckv = _rms(sm[:, :KV_RANK], kvg_ref[...])
    ckv_ref[...] = ckv.astype(BF16)
    ones_row = jnp.where(lax.broadcasted_iota(jnp.int32, (DSA_AUG, DSA_KC), 0) == 0,
                         jnp.float32(1.0), jnp.float32(0.0)).astype(BF16)
    for cc in range(tq // DSA_KC):
        ckvt_ref[cc, :KV_RANK, :] = jnp.transpose(ckv[cc * DSA_KC:(cc + 1) * DSA_KC]).astype(BF16)
        ckvt_ref[cc, KV_RANK:, :] = ones_row
    kidx_ref[...] = _rms(sm[:, KV_RANK:KV_RANK + IDX_DIM], kig_ref[...]).astype(BF16)
    tail = jnp.transpose(sm[:, KV_RANK:2 * KV_RANK])
    widxt_ref[...] = tail[IDX_DIM:IDX_DIM + IDX_HEADS, :] * (IDX_HEADS ** -0.5 * IDX_DIM ** -0.5)


def _dsa_attn_body(x_ref, qabs_ref, qidx_ref, widxt_ref, kidx_ref, ckv_ref, ckvt_ref, bias_ref,
                   tri_ref, wuv_ref, wout_ref, post_ref, o_ref,
                   keys_sc, hi_sc, lo_sc, acc_sc, m_sc, o_sc, am_sc, p_sc, alpha_sc, xt_sc, *, top_k):
    n = pl.program_id(1)
    kc = DSA_KC
    nch = (n + 2) // 2
    lane = lax.broadcasted_iota(jnp.int32, (kc, BLOCK), 1)
    row = lax.broadcasted_iota(jnp.int32, (kc, BLOCK), 0)
    qpos = n * BLOCK + lane
    nt = (((1,), (1,)), ((), ()))

    def scores(c, slot):
        xt_sc[slot] = lax.dot_general(kidx_ref[pl.ds(pl.multiple_of(c * kc, kc), kc), :], qidx_ref[...], nt,
                                      preferred_element_type=F32)

    def to_keys(c, slot):
        base = pl.multiple_of(c * kc, kc)
        acc = jnp.zeros((kc, BLOCK), F32)
        for h in range(IDX_HEADS):
            acc = acc + widxt_ref[h:h + 1, :] * jnp.maximum(xt_sc[slot, :, h * BLOCK:(h + 1) * BLOCK], 0.0)
        bits = pltpu.bitcast(acc, jnp.int32)
        key = bits ^ ((bits >> 31) & 0x7FFFFFFF)
        key = jnp.where(c * kc + row <= qpos, key, INT_MIN)
        keys_sc[pl.ds(base, kc), :] = key
        hi_sc[pl.ds(base, kc), :] = (key >> 16).astype(jnp.int16)

    def p1_pair(k, carry):
        c = 2 * k + 1
        to_keys(c - 1, 0)
        scores(c, 1)
        to_keys(c, 1)
        scores(c + 1, 0)
        return carry

    scores(0, 0)
    lax.fori_loop(0, (nch - 1) // 2, p1_pair, 0)

    @pl.when(nch % 2 == 0)
    def _():
        to_keys(nch - 2, 0)
        scores(nch - 1, 1)
        to_keys(nch - 1, 1)

    @pl.when(nch % 2 == 1)
    def _():
        to_keys(nch - 1, 0)

    def chunks(body, init):
        return lax.fori_loop(0, nch, lambda c, v: body(pl.multiple_of(c * kc, kc), v), init)

    def count_ge16(src_ref, cand):
        c16 = cand.astype(jnp.int16)

        def body(base, cnts):
            v = src_ref[pl.ds(base, kc), :]
            cnts = list(cnts)
            for j in range(kc // 16):
                a = j % COUNT_LANES
                cnts[a] = jnp.where(v[j * 16:(j + 1) * 16, :] >= c16, cnts[a] + jnp.int16(1), cnts[a])
            return tuple(cnts)

        cnts = chunks(body, (jnp.zeros((16, BLOCK), jnp.int16),) * COUNT_LANES)
        total = functools.reduce(lambda u, v: u + v, [c.astype(jnp.int32) for c in cnts])
        return jnp.sum(total, axis=0, keepdims=True, dtype=jnp.int32)

    def kth_largest16(src_ref, kth):
        def bit(it, u):
            cu = u | jnp.left_shift(jnp.int32(1), 15 - it)
            return jnp.where(count_ge16(src_ref, cu - 2 ** 15) >= kth, cu, u)
        return lax.fori_loop(0, 16, bit, jnp.zeros((1, BLOCK), jnp.int32)) - 2 ** 15

    def count_gt32(cand):
        def body(base, cnts):
            k = keys_sc[pl.ds(base, kc), :]
            cnts = list(cnts)
            for j in range(kc // 8):
                a = j % COUNT_LANES
                cnts[a] = jnp.where(k[j * 8:(j + 1) * 8, :] > cand, cnts[a] + 1, cnts[a])
            return tuple(cnts)
        cnts = chunks(body, (jnp.zeros((8, BLOCK), jnp.int32),) * COUNT_LANES)
        return jnp.sum(functools.reduce(lambda u, v: u + v, cnts), axis=0, keepdims=True, dtype=jnp.int32)

    hi = kth_largest16(hi_sc, top_k)
    above = jnp.where(hi >= 2 ** 15 - 1, 0, count_ge16(hi_sc, jnp.minimum(hi + 1, 2 ** 15 - 1)))

    def low_halves(base, carry):
        k = keys_sc[pl.ds(base, kc), :]
        lo = jnp.where((k >> 16) == hi, (k & 0xFFFF) - 2 ** 15, -2 ** 15)
        lo_sc[pl.ds(base, kc), :] = lo.astype(jnp.int16)
        return carry

    chunks(low_halves, 0)
    lo = kth_largest16(lo_sc, top_k - above)
    thr = hi * 2 ** 16 + (lo + 2 ** 15)
    need = (top_k - count_gt32(thr)).astype(F32)

    acc_sc[...] = jnp.zeros_like(acc_sc)
    m_sc[...] = jnp.full_like(m_sc, NEG_BIG)

    def probs(c, slot, eq_seen, far=False):
        base = pl.multiple_of(c * kc, kc)
        key = keys_sc[pl.ds(base, kc), :]
        eq = key == thr
        eqf = jnp.where(eq, jnp.float32(1.0), jnp.float32(0.0))
        rank = jnp.dot(tri_ref[...], eqf.astype(BF16), preferred_element_type=F32) + eq_seen
        sel = jnp.logical_or(key > thr, jnp.logical_and(eq, rank < need))
        sel = jnp.logical_and(sel, key != INT_MIN)
        am_sc[...] = jnp.where(sel, jnp.float32(0.0), jnp.float32(-jnp.inf))
        ckc = ckv_ref[pl.ds(base, kc), :]
        off = pl.multiple_of(2 * kc - jnp.minimum(n * BLOCK - c * kc, 2 * kc), BLOCK)
        for hp in range(N_HEADS // 2):
            lt2 = lax.dot_general(ckc, qabs_ref[hp * 2 * BLOCK:(hp + 1) * 2 * BLOCK, :], nt,
                                  preferred_element_type=F32)
            for j in range(2):
                h = 2 * hp + j
                ls = slice(j * BLOCK, (j + 1) * BLOCK)
                if far:
                    lt = lt2[:, ls] + am_sc[...]
                else:
                    lt = lt2[:, ls] + (am_sc[...] + bias_ref[h, pl.ds(off, kc), :])
                m_old = m_sc[h]
                m_new = jnp.maximum(m_old, jnp.max(lt, axis=0, keepdims=True))
                m_sc[h] = m_new
                alpha_sc[slot, hp, :, ls] = jnp.exp2(m_old - m_new)
                p_sc[slot, hp, :, ls] = jnp.exp2(lt - m_new).astype(BF16)
        return eq_seen + jnp.sum(eqf, axis=0, keepdims=True)

    def accumulate(c, slot):
        ckt = ckvt_ref[c]
        for hp in range(N_HEADS // 2):
            acc_sc[hp] = alpha_sc[slot, hp] * acc_sc[hp] + jnp.dot(
                ckt, p_sc[slot, hp], preferred_element_type=F32)

    def step01(c, eq_seen, far=False):
        accumulate(c - 1, 0)
        return probs(c, 1, eq_seen, far)

    def step10(c, eq_seen, far=False):
        accumulate(c - 1, 1)
        return probs(c, 0, eq_seen, far)

    def pair(far, k, eq_seen):
        return step10(2 * k + 2, step01(2 * k + 1, eq_seen, far), far)

    nfar = jnp.maximum((n - 2) // 2, 0)
    kfar = jnp.maximum((nfar - 1) // 2, 0)
    eq_seen = probs(0, 0, jnp.zeros((1, BLOCK), F32))
    eq_seen = lax.fori_loop(0, kfar, functools.partial(pair, True), eq_seen)
    eq_seen = lax.fori_loop(kfar, (nch - 1) // 2, functools.partial(pair, False), eq_seen)

    @pl.when(nch % 2 == 0)
    def _():
        step01(nch - 1, eq_seen)
        accumulate(nch - 1, 1)

    @pl.when(nch % 2 == 1)
    def _():
        accumulate(nch - 1, 0)

    for hp in range(N_HEADS // 2):
        a = acc_sc[hp]
        ot = (a[:KV_RANK] * (1.0 / a[KV_RANK:KV_RANK + 1])).astype(BF16)
        for j in range(2):
            h = 2 * hp + j
            o_sc[h * HEAD_DIM:(h + 1) * HEAD_DIM, :] = jnp.dot(
                wuv_ref[h], ot[:, j * BLOCK:(j + 1) * BLOCK],
                preferred_element_type=F32).astype(BF16)
    mt = jnp.dot(wout_ref[...], o_sc[...], preferred_element_type=F32)
    o_ref[...] = x_ref[...] + _rms(jnp.transpose(mt), post_ref[...])


def _dsa(x, pre_g, post_g, w_in, kv_g, ki_g, w_uk, w_uv, w_out, rel_bias):
    B, S, D = x.shape
    qdim = N_HEADS * HEAD_DIM
    qidim = IDX_HEADS * IDX_DIM
    tq = min(DSA_TQ, S)
    top_k = min(TOPK_MAX, S // 4)
    assert S % tq == 0 and tq % DSA_KC == 0 and KV_RANK == BLOCK and DSA_KC == 2 * BLOCK
    c0, c1, c2, c3 = qdim, qdim + KV_RANK, qdim + KV_RANK + qidim, qdim + KV_RANK + qidim + IDX_DIM
    wq = w_in[:, :c0].astype(BF16)
    wqi = w_in[:, c1:c2].astype(BF16)
    wsm = jnp.concatenate([w_in[:, c0:c1], w_in[:, c2:c3], w_in[:, c3:]], axis=1)
    wsm = jnp.pad(wsm, ((0, 0), (0, DSA_SMALL - wsm.shape[1]))).astype(BF16)
    wuk = jnp.transpose(w_uk, (1, 2, 0)).astype(BF16)
    wuv = jnp.transpose(w_uv, (1, 2, 0)).astype(BF16)
    wout_t = jnp.transpose(w_out).astype(BF16)
    nblk = S // BLOCK
    raug = KV_RANK + DSA_AUG

    proj = pl.pallas_call(
        functools.partial(_dsa_proj_body, tq=tq),
        out_shape=[
            jax.ShapeDtypeStruct((B, nblk, N_HEADS * BLOCK, KV_RANK), BF16),
            jax.ShapeDtypeStruct((B, S, KV_RANK), BF16),
            jax.ShapeDtypeStruct((B, S // DSA_KC, raug, DSA_KC), BF16),
            jax.ShapeDtypeStruct((B, nblk, IDX_HEADS * BLOCK, IDX_DIM), BF16),
            jax.ShapeDtypeStruct((B, S, IDX_DIM), BF16),
            jax.ShapeDtypeStruct((B, IDX_HEADS, S), F32),
        ],
        grid=(B, S // tq),
        in_specs=[
            pl.BlockSpec((None, tq, D), lambda b, i: (b, i, 0)),
            _resident((1, D)),
            _resident((D, qdim)), _resident((D, qidim)), _resident((D, DSA_SMALL)),
            _resident((1, KV_RANK)), _resident((1, IDX_DIM)),
            _resident((N_HEADS, HEAD_DIM, KV_RANK)),
        ],
        out_specs=[
            pl.BlockSpec((None, tq // BLOCK, N_HEADS * BLOCK, KV_RANK), lambda b, i: (b, i, 0, 0)),
            pl.BlockSpec((None, tq, KV_RANK), lambda b, i: (b, i, 0)),
            pl.BlockSpec((None, tq // DSA_KC, raug, DSA_KC), lambda b, i: (b, i, 0, 0)),
            pl.BlockSpec((None, tq // BLOCK, IDX_HEADS * BLOCK, IDX_DIM), lambda b, i: (b, i, 0, 0)),
            pl.BlockSpec((None, tq, IDX_DIM), lambda b, i: (b, i, 0)),
            pl.BlockSpec((None, IDX_HEADS, tq), lambda b, i: (b, 0, i)),
        ],
        compiler_params=pltpu.CompilerParams(
            dimension_semantics=("parallel", "parallel"),
            vmem_limit_bytes=V7X_VMEM_LIMIT_BYTES),
        name="dsa_proj",
    )
    qabs, ckv, ckvt, qidx, kidx, widxt = proj(
        x, pre_g.reshape(1, D), wq, wqi, wsm, kv_g.reshape(1, KV_RANK), ki_g.reshape(1, IDX_DIM), wuk)

    far = rel_bias[REL_BUCKETS - 1].astype(F32)
    bias_t = (_bias_toeplitz(rel_bias, 2 * DSA_KC, 3 * DSA_KC, BLOCK) - far[:, None, None]) * LOG2E
    tri = jnp.tril(jnp.ones((DSA_KC, DSA_KC), F32), -1).astype(BF16)

    attn = pl.pallas_call(
        functools.partial(_dsa_attn_body, top_k=top_k),
        out_shape=jax.ShapeDtypeStruct((B, S, D), F32),
        grid=(B, nblk),
        in_specs=[
            pl.BlockSpec((None, BLOCK, D), lambda b, n: (b, n, 0)),
            pl.BlockSpec((None, None, N_HEADS * BLOCK, KV_RANK), lambda b, n: (b, n, 0, 0)),
            pl.BlockSpec((None, None, IDX_HEADS * BLOCK, IDX_DIM), lambda b, n: (b, n, 0, 0)),
            pl.BlockSpec((None, IDX_HEADS, BLOCK), lambda b, n: (b, 0, n)),
            pl.BlockSpec((None, S, IDX_DIM), lambda b, n: (b, 0, 0)),
            pl.BlockSpec((None, S, KV_RANK), lambda b, n: (b, 0, 0)),
            pl.BlockSpec((None, S // DSA_KC, raug, DSA_KC), lambda b, n: (b, 0, 0, 0)),
            _resident((N_HEADS, 3 * DSA_KC, BLOCK)),
            _resident((DSA_KC, DSA_KC)),
            _resident((N_HEADS, HEAD_DIM, KV_RANK)),
            _resident((D, qdim)),
            _resident((1, D)),
        ],
        out_specs=pl.BlockSpec((None, BLOCK, D), lambda b, n: (b, n, 0)),
        scratch_shapes=[
            pltpu.VMEM((S, BLOCK), jnp.int32),
            pltpu.VMEM((S, BLOCK), jnp.int16),
            pltpu.VMEM((S, BLOCK), jnp.int16),
            pltpu.VMEM((N_HEADS // 2, raug, 2 * BLOCK), F32),
            pltpu.VMEM((N_HEADS, 1, BLOCK), F32),
            pltpu.VMEM((qdim, BLOCK), BF16),
            pltpu.VMEM((DSA_KC, BLOCK), F32),
            pltpu.VMEM((2, N_HEADS // 2, DSA_KC, 2 * BLOCK), BF16),
            pltpu.VMEM((2, N_HEADS // 2, 1, 2 * BLOCK), F32),
            pltpu.VMEM((2, DSA_KC, IDX_HEADS * BLOCK), F32),
        ],
        compiler_params=pltpu.CompilerParams(
            dimension_semantics=("parallel", "parallel"),
            vmem_limit_bytes=V7X_VMEM_LIMIT_BYTES),
        name="dsa_attn",
    )
    return attn(x, qabs, qidx, widxt, kidx, ckv, ckvt, bias_t, tri, wuv, wout_t, post_g.reshape(1, D))


def kernel(x, rel_bias, l0_norm_pre_mix, l0_norm_post_mix, l0_attn_w_in, l0_attn_sinks, l0_attn_w_out, l0_norm_pre_ffn, l0_norm_post_ffn, l0_ffn_w_up, l0_ffn_conv_w, l0_ffn_conv_b, l0_ffn_w_down, l1_norm_pre_mix, l1_norm_post_mix, l1_gmlp_w_in, l1_gmlp_ln_g, l1_gmlp_ln_b, l1_gmlp_w_s, l1_gmlp_b_s, l1_gmlp_w_out, l1_norm_pre_ffn, l1_norm_post_ffn, l1_ffn_w_up, l1_ffn_conv_w, l1_ffn_conv_b, l1_ffn_w_down, l2_norm_pre_mix, l2_norm_post_mix, l2_dsa_w_in, l2_dsa_kv_norm, l2_dsa_kidx_norm, l2_dsa_w_uk, l2_dsa_w_uv, l2_dsa_w_out, l2_norm_pre_ffn, l2_norm_post_ffn, l2_ffn_w_up, l2_ffn_conv_w, l2_ffn_conv_b, l2_ffn_w_down, l3_norm_pre_mix, l3_norm_post_mix, l3_attn_w_in, l3_attn_sinks, l3_attn_w_out, l3_norm_pre_ffn, l3_norm_post_ffn, l3_ffn_w_up, l3_ffn_conv_w, l3_ffn_conv_b, l3_ffn_w_down):
    x = _swa(x, l0_norm_pre_mix, l0_norm_post_mix, l0_attn_w_in, l0_attn_sinks, l0_attn_w_out, rel_bias)
    x = _ffn(x, l0_norm_pre_ffn, l0_norm_post_ffn, l0_ffn_w_up, l0_ffn_conv_w, l0_ffn_conv_b, l0_ffn_w_down)
    x = _gmlp(x, l1_norm_pre_mix, l1_norm_post_mix, l1_gmlp_w_in, l1_gmlp_ln_g, l1_gmlp_ln_b, l1_gmlp_w_s, l1_gmlp_b_s, l1_gmlp_w_out)
    x = _ffn(x, l1_norm_pre_ffn, l1_norm_post_ffn, l1_ffn_w_up, l1_ffn_conv_w, l1_ffn_conv_b, l1_ffn_w_down)
    x = _dsa(x, l2_norm_pre_mix, l2_norm_post_mix, l2_dsa_w_in, l2_dsa_kv_norm, l2_dsa_kidx_norm, l2_dsa_w_uk, l2_dsa_w_uv, l2_dsa_w_out, rel_bias)
    x = _ffn(x, l2_norm_pre_ffn, l2_norm_post_ffn, l2_ffn_w_up, l2_ffn_conv_w, l2_ffn_conv_b, l2_ffn_w_down)
    x = _swa(x, l3_norm_pre_mix, l3_norm_post_mix, l3_attn_w_in, l3_attn_sinks, l3_attn_w_out, rel_bias)
    x = _ffn(x, l3_norm_pre_ffn, l3_norm_post_ffn, l3_ffn_w_up, l3_ffn_conv_w, l3_ffn_conv_b, l3_ffn_w_down)
    return x
```
